```python
import math
import jax, jax.numpy as jnp
from jax import lax
import numpy as np

D_MODEL = 1024
BATCH = 8
SEQ = 2048
DEPTH = 4
DEC_BATCH = 128
DEC_SEQ = 4
PAST_LEN = 16384
PAGE_SIZE = 128

N_MIXERS = 3
S5_GROUP = 16
S5_GROUPS = D_MODEL // S5_GROUP
S5_STATE = 64
S5_DT_MIN = 1e-3
S5_DT_MAX = 1e-1
RET_HEADS = 4
RET_QK = D_MODEL // RET_HEADS
RET_V = 2 * D_MODEL // RET_HEADS
RET_CHUNK = 128
ROPE_BASE = 10000.0
POOL_WINDOWS = (2, 4, 8, 16)
POOL_N = len(POOL_WINDOWS)
POOL_GROUP = D_MODEL // POOL_N
POOL_BUF = max(POOL_WINDOWS) - 1
D_FF = 4 * D_MODEL
EPS = 1e-6
N_S5 = len(range(0, DEPTH, N_MIXERS))
N_RET = len(range(1, DEPTH, N_MIXERS))
N_POOL = len(range(2, DEPTH, N_MIXERS))

kernel_name = 'hybrid_s5_retention_pool_step'


def rmsnorm(x, g):
    xf = x.astype(jnp.float32)
    y = xf * lax.rsqrt(jnp.mean(xf * xf, axis=-1, keepdims=True) + EPS)
    return (y * g.astype(jnp.float32)).astype(x.dtype)


def s5_mixer(h, st_re, st_im, a_re, a_im, log_dt, b_re, b_im, c_re, c_im, d_skip, w_glu):
    f32 = jnp.float32
    n, t, _ = h.shape
    hf = h.astype(f32)
    u = hf.reshape(n, t, S5_GROUPS, S5_GROUP)
    dt = jnp.exp(log_dt.astype(f32))[:, None]
    lr, li = a_re.astype(f32), a_im.astype(f32)
    mag = jnp.exp(lr * dt)
    ab_re, ab_im = mag * jnp.cos(li * dt), mag * jnp.sin(li * dt)
    den = lr * lr + li * li
    nr, ni = ab_re - 1.0, ab_im
    f_re = (nr * lr + ni * li) / den
    f_im = (ni * lr - nr * li) / den
    br, bi = b_re.astype(f32), b_im.astype(f32)
    bb_re = f_re[..., None] * br - f_im[..., None] * bi
    bb_im = f_re[..., None] * bi + f_im[..., None] * br
    bu_re = jnp.einsum('gph,ntgh->ntgp', bb_re, u)
    bu_im = jnp.einsum('gph,ntgh->ntgp', bb_im, u)
    s0r, s0i = st_re.astype(f32), st_im.astype(f32)
    bu_re = bu_re.at[:, 0].add(ab_re * s0r - ab_im * s0i)
    bu_im = bu_im.at[:, 0].add(ab_re * s0i + ab_im * s0r)
    a_r = jnp.broadcast_to(ab_re, bu_re.shape)
    a_i = jnp.broadcast_to(ab_im, bu_im.shape)

    def combine(e1, e2):
        a1r, a1i, b1r, b1i = e1
        a2r, a2i, b2r, b2i = e2
        return (a1r * a2r - a1i * a2i, a1r * a2i + a1i * a2r,
                a2r * b1r - a2i * b1i + b2r, a2r * b1i + a2i * b1r + b2i)

    _, _, sr, si = lax.associative_scan(combine, (a_r, a_i, bu_re, bu_im), axis=1)
    y = (jnp.einsum('ghp,ntgp->ntgh', c_re.astype(f32), sr)
         - jnp.einsum('ghp,ntgp->ntgh', c_im.astype(f32), si))
    y = y.reshape(n, t, D_MODEL) + d_skip.astype(f32) * hf
    g = jax.nn.gelu(y).astype(h.dtype)
    ab = g @ w_glu
    out = ab[..., :D_MODEL] * jax.nn.sigmoid(ab[..., D_MODEL:])
    return out, sr[:, -1], si[:, -1]


def rotary(x, pos):
    half = RET_QK // 2
    inv = ROPE_BASE ** (-jnp.arange(half, dtype=jnp.float32) / half)
    ang = pos.astype(jnp.float32)[:, None] * inv[None, :]
    cos = jnp.cos(ang)[None, :, None, :]
    sin = jnp.sin(ang)[None, :, None, :]
    x1, x2 = x[..., :half], x[..., half:]
    return jnp.concatenate([x1 * cos - x2 * sin, x1 * sin + x2 * cos], axis=-1)


def retention_mixer(h, state, pos0, w_in, w_out):
    f32 = jnp.float32
    n, t, _ = h.shape
    proj = h @ w_in
    q = proj[..., :D_MODEL].astype(f32).reshape(n, t, RET_HEADS, RET_QK)
    k = proj[..., D_MODEL:2 * D_MODEL].astype(f32).reshape(n, t, RET_HEADS, RET_QK)
    v = proj[..., 2 * D_MODEL:4 * D_MODEL].astype(f32).reshape(n, t, RET_HEADS, RET_V)
    gate = proj[..., 4 * D_MODEL:]
    pos = pos0 + jnp.arange(t)
    q = rotary(q, pos) * (RET_QK ** -0.5)
    k = rotary(k, pos)
    log_g = jnp.log1p(-jnp.exp2(-5.0 - jnp.arange(RET_HEADS, dtype=f32)))
    L = math.gcd(t, RET_CHUNK)
    nc = t // L
    to_chunks = lambda z: z.reshape(n, nc, L, RET_HEADS, z.shape[-1]).transpose(1, 0, 2, 3, 4)
    qc, kc, vc = to_chunks(q), to_chunks(k), to_chunks(v)
    idx = jnp.arange(L, dtype=f32)
    diff = idx[:, None] - idx[None, :]
    dmask = jnp.exp(jnp.maximum(diff, 0.0)[None] * log_g[:, None, None]) * (diff >= 0)[None]
    q_dec = jnp.exp((idx + 1.0)[:, None] * log_g[None, :])
    k_dec = jnp.exp((L - 1.0 - idx)[:, None] * log_g[None, :])
    chunk_dec = jnp.exp(L * log_g)

    def step(S, blk):
        qb, kb, vb = blk
        scores = jnp.einsum('nihd,njhd->nhij', qb, kb) * dmask[None]
        o = (jnp.einsum('nhij,njhe->nihe', scores, vb)
             + jnp.einsum('nihd,nhde->nihe', qb * q_dec[None, :, :, None], S))
        S = (chunk_dec[None, :, None, None] * S
             + jnp.einsum('njhd,njhe->nhde', kb * k_dec[None, :, :, None], vb))
        return S, o

    S, o = lax.scan(step, state.astype(f32), (qc, kc, vc))
    o = o.transpose(1, 0, 2, 3, 4).reshape(n, t, RET_HEADS, RET_V)
    mu = jnp.mean(o, axis=-1, keepdims=True)
    oc = o - mu
    o = oc * lax.rsqrt(jnp.mean(oc * oc, axis=-1, keepdims=True) + EPS)
    o = o.reshape(n, t, 2 * D_MODEL).astype(h.dtype) * jax.nn.silu(gate)
    return o @ w_out, S


def pool_mixer(h, buf, pos0, w_grp, scale):
    f32 = jnp.float32
    n, t, _ = h.shape
    ext = jnp.concatenate([buf.astype(h.dtype), h], axis=1)
    cs = jnp.cumsum(ext.astype(f32), axis=1)
    cs = jnp.concatenate([jnp.zeros((n, 1, D_MODEL), f32), cs], axis=1)
    pos = pos0 + jnp.arange(t)
    outs = []
    for gi, w in enumerate(POOL_WINDOWS):
        lo, hi = gi * POOL_GROUP, (gi + 1) * POOL_GROUP
        end = cs[:, POOL_BUF + 1:POOL_BUF + 1 + t, lo:hi]
        start = cs[:, POOL_BUF + 1 - w:POOL_BUF + 1 - w + t, lo:hi]
        cnt = jnp.minimum(w, pos + 1).astype(f32)[None, :, None]
        outs.append((end - start) / cnt)
    pooled = jnp.concatenate(outs, axis=-1) - h.astype(f32)
    z = pooled.astype(h.dtype).reshape(n, t, POOL_N, POOL_GROUP)
    z = jnp.einsum('ntgc,gcd->ntgd', z, w_grp).reshape(n, t, D_MODEL)
    return z * scale, ext[:, -POOL_BUF:]


def trunk(x, pos0, st_s5_re, st_s5_im, st_ret, st_pool, norm_mix, norm_ffn, norm_final,
          s5_a_re, s5_a_im, s5_log_dt, s5_b_re, s5_b_im, s5_c_re, s5_c_im, s5_d, s5_w_glu,
          ret_w_in, ret_w_out, pool_w, pool_scale, mlp_w_up, mlp_w_down):
    new_re, new_im, new_ret, new_pool = [], [], [], []
    for layer in range(DEPTH):
        kind = layer % N_MIXERS
        j = layer // N_MIXERS
        h = rmsnorm(x, norm_mix[layer])
        if kind == 0:
            mix, sr, si = s5_mixer(h, st_s5_re[j], st_s5_im[j], s5_a_re[j], s5_a_im[j], s5_log_dt[j],
                                   s5_b_re[j], s5_b_im[j], s5_c_re[j], s5_c_im[j], s5_d[j], s5_w_glu[j])
            new_re.append(sr.astype(st_s5_re.dtype))
            new_im.append(si.astype(st_s5_im.dtype))
        elif kind == 1:
            mix, s = retention_mixer(h, st_ret[j], pos0, ret_w_in[j], ret_w_out[j])
            new_ret.append(s.astype(st_ret.dtype))
        else:
            mix, b = pool_mixer(h, st_pool[j], pos0, pool_w[j], pool_scale[j])
            new_pool.append(b.astype(st_pool.dtype))
        x = x + mix.astype(x.dtype)
        h = rmsnorm(x, norm_ffn[layer])
        x = x + (jnp.square(jax.nn.relu(h @ mlp_w_up[layer])) @ mlp_w_down[layer]).astype(x.dtype)
    y = rmsnorm(x, norm_final)
    return y, jnp.stack(new_re), jnp.stack(new_im), jnp.stack(new_ret), jnp.stack(new_pool)


def setup_inputs(seed: int = 0) -> dict:
    key = jax.random.key(seed)
    ks = jax.random.split(key, 24)
    f32 = jnp.float32
    nrm = lambda k, shape, s: jax.random.normal(k, shape, f32) * s
    n_idx = jnp.arange(S5_STATE, dtype=f32) * math.pi
    return {
        'x_prompt': nrm(ks[0], (BATCH, SEQ, D_MODEL), 1.0),
        'x_sample': nrm(ks[1], (DEC_BATCH, DEC_SEQ, D_MODEL), 1.0),
        'state_s5_re': nrm(ks[2], (N_S5, DEC_BATCH, S5_GROUPS, S5_STATE), 0.3),
        'state_s5_im': nrm(ks[3], (N_S5, DEC_BATCH, S5_GROUPS, S5_STATE), 0.3),
        'state_ret': nrm(ks[4], (N_RET, DEC_BATCH, RET_HEADS, RET_QK, RET_V), 4.0),
        'state_pool': nrm(ks[5], (N_POOL, DEC_BATCH, POOL_BUF, D_MODEL), 1.0),
        'norm_mix': 1.0 + nrm(ks[6], (DEPTH, D_MODEL), 0.02),
        'norm_ffn': 1.0 + nrm(ks[7], (DEPTH, D_MODEL), 0.02),
        'norm_final': 1.0 + nrm(ks[8], (D_MODEL,), 0.02),
        's5_a_re': -0.5 + nrm(ks[9], (N_S5, S5_GROUPS, S5_STATE), 0.01),
        's5_a_im': n_idx + nrm(ks[10], (N_S5, S5_GROUPS, S5_STATE), 0.01),
        's5_log_dt': jax.random.uniform(ks[11], (N_S5, S5_GROUPS), f32,
                                        math.log(S5_DT_MIN), math.log(S5_DT_MAX)),
        's5_b_re': nrm(ks[12], (N_S5, S5_GROUPS, S5_STATE, S5_GROUP), 0.5),
        's5_b_im': nrm(ks[13], (N_S5, S5_GROUPS, S5_STATE, S5_GROUP), 0.5),
        's5_c_re': nrm(ks[14], (N_S5, S5_GROUPS, S5_GROUP, S5_STATE), S5_STATE ** -0.5),
        's5_c_im': nrm(ks[15], (N_S5, S5_GROUPS, S5_GROUP, S5_STATE), S5_STATE ** -0.5),
        's5_d': nrm(ks[16], (N_S5, D_MODEL), 1.0),
        's5_w_glu': nrm(ks[17], (N_S5, D_MODEL, 2 * D_MODEL), D_MODEL ** -0.5),
        'ret_w_in': nrm(ks[18], (N_RET, D_MODEL, 6 * D_MODEL), D_MODEL ** -0.5),
        'ret_w_out': nrm(ks[19], (N_RET, 2 * D_MODEL, D_MODEL), (2 * D_MODEL) ** -0.5),
        'pool_w': nrm(ks[20], (N_POOL, POOL_N, POOL_GROUP, POOL_GROUP), POOL_GROUP ** -0.5),
        'pool_scale': 1.0 + nrm(ks[21], (N_POOL, D_MODEL), 0.02),
        'mlp_w_up': nrm(ks[22], (DEPTH, D_MODEL, D_FF), D_MODEL ** -0.5),
        'mlp_w_down': nrm(ks[23], (DEPTH, D_FF, D_MODEL), 0.5 * D_FF ** -0.5),
    }


def reference(x_prompt, x_sample, state_s5_re, state_s5_im, state_ret, state_pool,
              norm_mix, norm_ffn, norm_final, s5_a_re, s5_a_im, s5_log_dt, s5_b_re, s5_b_im,
              s5_c_re, s5_c_im, s5_d, s5_w_glu, ret_w_in, ret_w_out, pool_w, pool_scale,
              mlp_w_up, mlp_w_down):
    nb = x_prompt.shape[0]
    z_re = jnp.zeros((N_S5, nb, S5_GROUPS, S5_STATE), state_s5_re.dtype)
    z_im = jnp.zeros((N_S5, nb, S5_GROUPS, S5_STATE), state_s5_im.dtype)
    z_ret = jnp.zeros((N_RET, nb, RET_HEADS, RET_QK, RET_V), state_ret.dtype)
    z_pool = jnp.zeros((N_POOL, nb, POOL_BUF, D_MODEL), state_pool.dtype)
    y_prompt, p_re, p_im, p_ret, p_pool = trunk(
        x_prompt, 0, z_re, z_im, z_ret, z_pool, norm_mix, norm_ffn, norm_final,
        s5_a_re, s5_a_im, s5_log_dt, s5_b_re, s5_b_im, s5_c_re, s5_c_im, s5_d, s5_w_glu,
        ret_w_in, ret_w_out, pool_w, pool_scale, mlp_w_up, mlp_w_down)
    y_sample, s_re, s_im, s_ret, s_pool = trunk(
        x_sample, PAST_LEN, state_s5_re, state_s5_im, state_ret, state_pool, norm_mix, norm_ffn, norm_final,
        s5_a_re, s5_a_im, s5_log_dt, s5_b_re, s5_b_im, s5_c_re, s5_c_im, s5_d, s5_w_glu,
        ret_w_in, ret_w_out, pool_w, pool_scale, mlp_w_up, mlp_w_down)
    return (y_prompt, y_sample, p_re, p_im, p_ret, p_pool, s_re, s_im, s_ret, s_pool)
```

```python
import functools
import math

import jax
import jax.numpy as jnp
from jax import lax
from jax.experimental import pallas as pl
from jax.experimental.pallas import tpu as pltpu

F32 = jnp.float32
BF16 = jnp.bfloat16

EPS = 1e-6
S5_GROUP = 16
S5_STATE = 64
RET_HEADS = 4
RET_CHUNK = 128
ROPE_BASE = 10000.0
POOL_WINDOWS = (2, 4, 8, 16)
POOL_TAIL = 16

SUBLANES = 8
MXU_DIM = 256
ROW_TILE = 512
SCAN_LANES = 512
VMEM_LIMIT = 56 * 1024 * 1024
PAST_LEN = 16384


def _params(n_axes):
    return pltpu.CompilerParams(dimension_semantics=("arbitrary",) * n_axes, vmem_limit_bytes=VMEM_LIMIT)


def _const_spec(shape):
    zeros = (0,) * len(shape)
    return pl.BlockSpec(shape, lambda *_: zeros, pipeline_mode=pl.Buffered(1))


def _rms(x, g):
    ms = jnp.mean(x * x, axis=-1, keepdims=True)
    return x * lax.rsqrt(ms + EPS) * g


def _mlp_body(x_ref, g_ref, wu_ref, wd_ref, gf_ref, o_ref, *, final, ff_chunk):
    x = x_ref[...]
    h = _rms(x, g_ref[...]).astype(BF16)
    acc = jnp.zeros_like(x)
    d_ff = wu_ref.shape[1]
    for c in range(d_ff // ff_chunk):
        cols = slice(c * ff_chunk, (c + 1) * ff_chunk)
        a = jnp.dot(h, wu_ref[:, cols], preferred_element_type=F32)
        a = jnp.square(jnp.maximum(a, 0.0)).astype(BF16)
        acc = acc + jnp.dot(a, wd_ref[cols, :], preferred_element_type=F32)
    y = x + acc
    if final:
        y = _rms(y, gf_ref[...])
    o_ref[...] = y


def _mlp(x, g, w_up, w_down, g_final, final):
    rows, d = x.shape
    d_ff = w_up.shape[1]
    return pl.pallas_call(
        functools.partial(_mlp_body, final=final, ff_chunk=1024),
        grid=(rows // ROW_TILE,),
        in_specs=[pl.BlockSpec((ROW_TILE, d), lambda i: (i, 0)),
                  _const_spec((1, d)), _const_spec((d, d_ff)), _const_spec((d_ff, d)), _const_spec((1, d))],
        out_specs=pl.BlockSpec((ROW_TILE, d), lambda i: (i, 0)),
        out_shape=jax.ShapeDtypeStruct((rows, d), F32),
        compiler_params=_params(1),
        name="mlp_final" if final else "mlp",
    )(x, g.reshape(1, d), w_up, w_down, g_final.reshape(1, d))


def _mm_res_body(a_ref, w_ref, x_ref, o_ref, *, glu):
    ab = jnp.dot(a_ref[...], w_ref[...], preferred_element_type=F32)
    if glu:
        d = x_ref.shape[1]
        ab = ab[:, :d] * jax.nn.sigmoid(ab[:, d:])
    o_ref[...] = x_ref[...] + ab


def _mm_res(a, w, x, glu):
    rows, d = x.shape
    k = a.shape[1]
    return pl.pallas_call(
        functools.partial(_mm_res_body, glu=glu),
        grid=(rows // ROW_TILE,),
        in_specs=[pl.BlockSpec((ROW_TILE, k), lambda i: (i, 0)), _const_spec(w.shape),
                  pl.BlockSpec((ROW_TILE, d), lambda i: (i, 0))],
        out_specs=pl.BlockSpec((ROW_TILE, d), lambda i: (i, 0)),
        out_shape=jax.ShapeDtypeStruct((rows, d), F32),
        compiler_params=_params(1),
        name="glu_residual" if glu else "proj_residual",
    )(a, w, x)


def _s5_body(x_ref, g_ref, bw_ref, are_ref, aim_ref, cre_ref, cim_ref, dsk_ref, s0re_ref, s0im_ref,
             gl_ref, sre_out, sim_out, bu_re, bu_im, st_re, st_im, *, nb, tc):
    i = pl.program_id(0)

    @pl.when(i == 0)
    def _():
        st_re[...] = s0re_ref[...]
        st_im[...] = s0im_ref[...]

    h = _rms(x_ref[...], g_ref[...])
    hb = h.astype(BF16)
    n_slab = bw_ref.shape[0]
    n_state_slab = bw_ref.shape[2] // 2
    for j in range(n_slab):
        r = jnp.dot(hb[:, j * MXU_DIM:(j + 1) * MXU_DIM], bw_ref[j], preferred_element_type=F32)
        cols = slice(j * n_state_slab, (j + 1) * n_state_slab)
        bu_re[:, cols] = r[:, :n_state_slab]
        bu_im[:, cols] = r[:, n_state_slab:]

    n_lanes = bu_re.shape[1]
    for c in range(n_lanes // SCAN_LANES):
        lanes = slice(c * SCAN_LANES, (c + 1) * SCAN_LANES)
        ar = jnp.broadcast_to(are_ref[:, lanes], (SUBLANES, SCAN_LANES))
        ai = jnp.broadcast_to(aim_ref[:, lanes], (SUBLANES, SCAN_LANES))

        def row_group(rg, carry, lanes=lanes, ar=ar, ai=ai):
            g0 = pl.multiple_of(rg * SUBLANES, SUBLANES)
            sr = st_re[pl.ds(g0, SUBLANES), lanes]
            si = st_im[pl.ds(g0, SUBLANES), lanes]

            def step(t, s):
                sr, si = s
                r0 = pl.multiple_of(t * nb + g0, SUBLANES)
                nr = ar * sr - ai * si + bu_re[pl.ds(r0, SUBLANES), lanes]
                ni = ar * si + ai * sr + bu_im[pl.ds(r0, SUBLANES), lanes]
                bu_re[pl.ds(r0, SUBLANES), lanes] = nr
                bu_im[pl.ds(r0, SUBLANES), lanes] = ni
                return nr, ni

            sr, si = lax.fori_loop(0, tc, step, (sr, si), unroll=min(tc, 8))
            st_re[pl.ds(g0, SUBLANES), lanes] = sr
            st_im[pl.ds(g0, SUBLANES), lanes] = si
            return carry

        lax.fori_loop(0, nb // SUBLANES, row_group, 0)

    k_slab = cre_ref.shape[1]
    ys = []
    for j in range(cre_ref.shape[0]):
        cols = slice(j * k_slab, (j + 1) * k_slab)
        y = jnp.dot(bu_re[:, cols].astype(BF16), cre_ref[j], preferred_element_type=F32)
        y = y + jnp.dot(bu_im[:, cols].astype(BF16), cim_ref[j], preferred_element_type=F32)
        ys.append(y)
    y = jnp.concatenate(ys, axis=1) + dsk_ref[...] * h
    gl_ref[...] = jax.nn.gelu(y, approximate=True).astype(BF16)

    @pl.when(i == pl.num_programs(0) - 1)
    def _():
        sre_out[...] = st_re[...]
        sim_out[...] = st_im[...]


def _s5_core(xt, g, bw, a_re, a_im, c_re, c_im, d_skip, s0_re, s0_im, nb, tc):
    rows, d = xt.shape
    n_lanes = a_re.shape[1]
    r = tc * nb
    return pl.pallas_call(
        functools.partial(_s5_body, nb=nb, tc=tc),
        grid=(rows // r,),
        in_specs=[pl.BlockSpec((r, d), lambda i: (i, 0)),
                  _const_spec((1, d)), _const_spec(bw.shape), _const_spec((1, n_lanes)), _const_spec((1, n_lanes)),
                  _const_spec(c_re.shape), _const_spec(c_im.shape), _const_spec((1, d)),
                  _const_spec((nb, n_lanes)), _const_spec((nb, n_lanes))],
        out_specs=[pl.BlockSpec((r, d), lambda i: (i, 0)),
                   pl.BlockSpec((nb, n_lanes), lambda i: (0, 0)), pl.BlockSpec((nb, n_lanes), lambda i: (0, 0))],
        out_shape=[jax.ShapeDtypeStruct((rows, d), BF16),
                   jax.ShapeDtypeStruct((nb, n_lanes), F32), jax.ShapeDtypeStruct((nb, n_lanes), F32)],
        scratch_shapes=[pltpu.VMEM((r, n_lanes), F32), pltpu.VMEM((r, n_lanes), F32),
                        pltpu.VMEM((nb, n_lanes), F32), pltpu.VMEM((nb, n_lanes), F32)],
        compiler_params=_params(1),
        name="s5_core",
    )(xt, g.reshape(1, d), bw, a_re, a_im, c_re, c_im, d_skip.reshape(1, d), s0_re, s0_im)


def _s5_weights(a_re, a_im, log_dt, b_re, b_im, c_re, c_im):
    groups, p, hch = b_re.shape
    dt = jnp.exp(log_dt.astype(F32))[:, None]
    lr, li = a_re.astype(F32), a_im.astype(F32)
    mag = jnp.exp(lr * dt)
    ab_re, ab_im = mag * jnp.cos(li * dt), mag * jnp.sin(li * dt)
    den = lr * lr + li * li
    nr, ni = ab_re - 1.0, ab_im
    f_re = (nr * lr + ni * li) / den
    f_im = (ni * lr - nr * li) / den
    br, bi = b_re.astype(F32), b_im.astype(F32)
    bb_re = f_re[..., None] * br - f_im[..., None] * bi
    bb_im = f_re[..., None] * bi + f_im[..., None] * br
    gs = MXU_DIM // hch
    n_slab = groups // gs
    eye = jnp.eye(gs, dtype=F32)

    def pack_b(bb):
        bb = bb.reshape(n_slab, gs, p, hch)
        return jnp.einsum('sgph,gk->sghkp', bb, eye).reshape(n_slab, gs * hch, gs * p)

    def pack_c(cc):
        cc = cc.reshape(n_slab, gs, hch, p)
        return jnp.einsum('sghp,gk->sgpkh', cc, eye).reshape(n_slab, gs * p, gs * hch)

    bw = jnp.concatenate([pack_b(bb_re), pack_b(bb_im)], axis=2).astype(BF16)
    cw_re = pack_c(c_re.astype(F32)).astype(BF16)
    cw_im = pack_c(-c_im.astype(F32)).astype(BF16)
    return bw, ab_re.reshape(1, groups * p), ab_im.reshape(1, groups * p), cw_re, cw_im


def _ret_in_body(x_ref, g_ref, w_ref, cos_ref, sin_ref, q_ref, k_ref, v_ref, gate_ref, *, d, qk):
    h = _rms(x_ref[...], g_ref[...]).astype(BF16)
    cos, sin = cos_ref[...], sin_ref[...]
    half = qk // 2

    def rot(z):
        outs = []
        for hd in range(d // qk):
            z1 = z[:, hd * qk:hd * qk + half]
            z2 = z[:, hd * qk + half:(hd + 1) * qk]
            outs += [z1 * cos - z2 * sin, z1 * sin + z2 * cos]
        return jnp.concatenate(outs, axis=1)

    q = jnp.dot(h, w_ref[:, 0:d], preferred_element_type=F32)
    q_ref[...] = rot(q) * (qk ** -0.5)
    k = jnp.dot(h, w_ref[:, d:2 * d], preferred_element_type=F32)
    k_ref[...] = rot(k)
    v_ref[...] = jnp.dot(h, w_ref[:, 2 * d:4 * d], preferred_element_type=F32)
    gate_ref[...] = jnp.dot(h, w_ref[:, 4 * d:6 * d], preferred_element_type=F32)


def _ret_in(x, g, w_in, cos_tab, sin_tab):
    rows, d = x.shape
    qk = d // RET_HEADS
    tab_blocks = cos_tab.shape[0] // ROW_TILE
    row_spec = lambda width: pl.BlockSpec((ROW_TILE, width), lambda i: (i, 0))
    tab_spec = pl.BlockSpec((ROW_TILE, qk // 2), lambda i: (i % tab_blocks, 0))
    return pl.pallas_call(
        functools.partial(_ret_in_body, d=d, qk=qk),
        grid=(rows // ROW_TILE,),
        in_specs=[row_spec(d), _const_spec((1, d)), _const_spec(w_in.shape), tab_spec, tab_spec],
        out_specs=[row_spec(d), row_spec(d), row_spec(2 * d), row_spec(2 * d)],
        out_shape=[jax.ShapeDtypeStruct((rows, d), F32), jax.ShapeDtypeStruct((rows, d), F32),
                   jax.ShapeDtypeStruct((rows, 2 * d), F32), jax.ShapeDtypeStruct((rows, 2 * d), F32)],
        compiler_params=_params(1),
        name="ret_in",
    )(x, g.reshape(1, d), w_in, cos_tab, sin_tab)


def _ret_core_body(q_ref, k_ref, v_ref, gate_ref, s0_ref, dmask_ref, qdec_ref, kdec_ref, cdec_ref,
                   og_ref, s_ref, *, lb, lp, qk, vd):
    c = pl.program_id(1)

    @pl.when(c == 0)
    def _():
        s_ref[...] = s0_ref[...]

    def padded(z):
        if lb == lp:
            return z
        return jnp.concatenate([z, jnp.zeros((lp - lb, z.shape[1]), z.dtype)], axis=0)

    q, k, v = padded(q_ref[...]), padded(k_ref[...]), padded(v_ref[...])
    gate = padded(gate_ref[...])
    for hd in range(RET_HEADS):
        qh = q[:, hd * qk:(hd + 1) * qk].astype(BF16)
        kf = k[:, hd * qk:(hd + 1) * qk]
        kh = kf.astype(BF16)
        vh = v[:, hd * vd:(hd + 1) * vd].astype(BF16)
        s_old = s_ref[hd]
        scores = lax.dot_general(qh, kh, (((1,), (1,)), ((), ())), preferred_element_type=F32) * dmask_ref[hd]
        o = jnp.dot(scores.astype(BF16), vh, preferred_element_type=F32)
        o = o + qdec_ref[hd] * jnp.dot(qh, s_old.astype(BF16), preferred_element_type=F32)
        kd = (kf * kdec_ref[hd]).astype(BF16)
        s_ref[hd] = cdec_ref[hd] * s_old + lax.dot_general(kd, vh, (((0,), (0,)), ((), ())),
                                                           preferred_element_type=F32)
        mu = jnp.mean(o, axis=-1, keepdims=True)
        oc = o - mu
        o = oc * lax.rsqrt(jnp.mean(oc * oc, axis=-1, keepdims=True) + EPS)
        og = o * jax.nn.silu(gate[:, hd * vd:(hd + 1) * vd])
        og_ref[:, hd * vd:(hd + 1) * vd] = og[:lb].astype(BF16)


def _ret_core(q, k, v, gate, s0, consts, n, nc, lb):
    rows, d = q.shape
    qk, vd = d // RET_HEADS, 2 * d // RET_HEADS
    dmask, qdec, kdec, cdec = consts
    lp = dmask.shape[1]
    row_spec = lambda width: pl.BlockSpec((lb, width), lambda b, c: (b * nc + c, 0))
    s_spec = pl.BlockSpec((None, RET_HEADS, qk, vd), lambda b, c: (b, 0, 0, 0))
    return pl.pallas_call(
        functools.partial(_ret_core_body, lb=lb, lp=lp, qk=qk, vd=vd),
        grid=(n, nc),
        in_specs=[row_spec(d), row_spec(d), row_spec(2 * d), row_spec(2 * d), s_spec,
                  _const_spec(dmask.shape), _const_spec(qdec.shape), _const_spec(kdec.shape), _const_spec(cdec.shape)],
        out_specs=[row_spec(2 * d), s_spec],
        out_shape=[jax.ShapeDtypeStruct((rows, 2 * d), BF16), jax.ShapeDtypeStruct(s0.shape, F32)],
        compiler_params=_params(2),
        name="ret_core",
    )(q, k, v, gate, s0, dmask, qdec, kdec, cdec)


def _ret_consts(l_real, l_pad, qk, vd):
    log_g = jnp.log1p(-jnp.exp2(-5.0 - jnp.arange(RET_HEADS, dtype=F32)))
    idx = jnp.arange(l_real, dtype=F32)
    diff = idx[:, None] - idx[None, :]
    dmask = jnp.exp(jnp.maximum(diff, 0.0)[None] * log_g[:, None, None]) * (diff >= 0)[None]
    q_dec = jnp.exp((idx + 1.0)[:, None] * log_g[None, :])
    k_dec = jnp.exp((l_real - 1.0 - idx)[:, None] * log_g[None, :])
    chunk_dec = jnp.exp(l_real * log_g)
    pad = l_pad - l_real
    dmask = jnp.pad(dmask, ((0, 0), (0, pad), (0, pad)))
    q_dec = jnp.pad(q_dec.T, ((0, 0), (0, pad)))
    k_dec = jnp.pad(k_dec.T, ((0, 0), (0, pad)))
    qdec = jnp.broadcast_to(q_dec[:, :, None], (RET_HEADS, l_pad, vd))
    kdec = jnp.broadcast_to(k_dec[:, :, None], (RET_HEADS, l_pad, qk))
    cdec = jnp.broadcast_to(chunk_dec[:, None, None], (RET_HEADS, 1, vd))
    return dmask, qdec, kdec, cdec


def _rope_tables(pos0, t, qk):
    half = qk // 2
    inv = ROPE_BASE ** (-jnp.arange(half, dtype=F32) / half)
    pos = pos0 + jnp.arange(t)
    ang = pos.astype(F32)[:, None] * inv[None, :]
    return jnp.cos(ang), jnp.sin(ang)


def _pool_body(x_ref, g_ref, buf_ref, w_ref, sc_ref, o_ref, tail_out, tail, *, nb, tc, pos0):
    i = pl.program_id(0)

    @pl.when(i == 0)
    def _():
        tail[...] = buf_ref[...]

    x = x_ref[...]
    h = _rms(x, g_ref[...])
    ext = jnp.concatenate([tail[...], h], axis=0)
    rows = tc * nb
    grp = x.shape[1] // len(POOL_WINDOWS)
    t_idx = i * tc + lax.shift_right_logical(lax.broadcasted_iota(jnp.int32, (rows, grp), 0), nb.bit_length() - 1)
    for gi, w in enumerate(POOL_WINDOWS):
        cols = slice(gi * grp, (gi + 1) * grp)
        s, first, span = ext[:, cols], 0, 1
        while span < w:
            s = s[span * nb:] + s[:-span * nb]
            first += span
            span *= 2
        s = s[(POOL_TAIL - first) * nb:]
        cnt = jnp.minimum(w, pos0 + t_idx + 1).astype(F32)
        pooled = s / cnt - h[:, cols]
        z = jnp.dot(pooled.astype(BF16), w_ref[gi], preferred_element_type=F32)
        o_ref[:, cols] = x[:, cols] + z * sc_ref[:, cols]
    new_tail = ext[rows:]
    tail[...] = new_tail

    @pl.when(i == pl.num_programs(0) - 1)
    def _():
        tail_out[...] = new_tail


def _pool(xt, g, buf_t, w, scale, nb, tc, pos0):
    rows, d = xt.shape
    assert nb & (nb - 1) == 0
    r = tc * nb
    tail_rows = POOL_TAIL * nb
    return pl.pallas_call(
        functools.partial(_pool_body, nb=nb, tc=tc, pos0=pos0),
        grid=(rows // r,),
        in_specs=[pl.BlockSpec((r, d), lambda i: (i, 0)), _const_spec((1, d)), _const_spec((tail_rows, d)),
                  _const_spec(w.shape), _const_spec((1, d))],
        out_specs=[pl.BlockSpec((r, d), lambda i: (i, 0)), pl.BlockSpec((tail_rows, d), lambda i: (0, 0))],
        out_shape=[jax.ShapeDtypeStruct((rows, d), F32), jax.ShapeDtypeStruct((tail_rows, d), F32)],
        scratch_shapes=[pltpu.VMEM((tail_rows, d), F32)],
        compiler_params=_params(1),
        name="pool_mixer",
    )(xt, g.reshape(1, d), buf_t, w, scale.reshape(1, d))


def _to_time_major(xb, n, t):
    return xb.reshape(n, t, -1).transpose(1, 0, 2).reshape(n * t, -1)


def _to_batch_major(xt, n, t):
    return xt.reshape(t, n, -1).transpose(1, 0, 2).reshape(n * t, -1)


def _trunk(x, pos0, st_re, st_im, st_ret, st_pool, wts):
    n, t, d = x.shape
    rows = n * t
    depth = wts['norm_mix'].shape[0]
    qk, vd = d // RET_HEADS, 2 * d // RET_HEADS
    tc = ROW_TILE // n if n < ROW_TILE else 1
    tc = min(tc, t)
    new_re, new_im, new_ret, new_pool = [], [], [], []

    xb = x.reshape(rows, d)
    time_major = False
    for layer in range(depth):
        kind, j = layer % 3, layer // 3
        if kind == 1:
            if time_major:
                xb, time_major = _to_batch_major(xb, n, t), False
            l_real = math.gcd(t, RET_CHUNK)
            nc = t // l_real
            lb = max(l_real, SUBLANES)
            cos, sin = _rope_tables(pos0, t, qk)
            if t < ROW_TILE:
                cos, sin = jnp.tile(cos, (ROW_TILE // t, 1)), jnp.tile(sin, (ROW_TILE // t, 1))
            q, k, v, gate = _ret_in(xb, wts['norm_mix'][layer], wts['ret_w_in'][j], cos, sin)
            if lb != l_real:
                pad = lambda z: jnp.pad(z.reshape(n * nc, l_real, -1), ((0, 0), (0, lb - l_real), (0, 0))
                                        ).reshape(n * nc * lb, -1)
                q, k, v, gate = pad(q), pad(k), pad(v), pad(gate)
            consts = _ret_consts(l_real, RET_CHUNK, qk, vd)
            og, s_new = _ret_core(q, k, v, gate, st_ret[j], consts, n, nc, lb)
            if lb != l_real:
                og = og.reshape(n * nc, lb, -1)[:, :l_real].reshape(rows, -1)
            xb = _mm_res(og, wts['ret_w_out'][j], xb, glu=False)
            new_ret.append(s_new)
        else:
            if not time_major:
                xb, time_major = _to_time_major(xb, n, t), True
            if kind == 0:
                bw, a_re, a_im, cw_re, cw_im = wts['s5_packed'][j]
                gl, s_re, s_im = _s5_core(xb, wts['norm_mix'][layer], bw, a_re, a_im, cw_re, cw_im,
                                          wts['s5_d'][j], st_re[j].reshape(n, -1), st_im[j].reshape(n, -1), n, tc)
                xb = _mm_res(gl, wts['s5_w_glu'][j], xb, glu=True)
                new_re.append(s_re.reshape(st_re[j].shape))
                new_im.append(s_im.reshape(st_im[j].shape))
            else:
                buf = st_pool[j]
                buf_t = jnp.pad(buf.transpose(1, 0, 2), ((1, 0), (0, 0), (0, 0))).reshape(POOL_TAIL * n, d)
                xb, tail = _pool(xb, wts['norm_mix'][layer], buf_t, wts['pool_w'][j], wts['pool_scale'][j],
                                 n, tc, pos0)
                new_pool.append(tail.reshape(POOL_TAIL, n, d)[1:].transpose(1, 0, 2))
        final = layer == depth - 1
        xb = _mlp(xb, wts['norm_ffn'][layer], wts['mlp_w_up'][layer], wts['mlp_w_down'][layer],
                  wts['norm_final'], final)
    if time_major:
        xb = _to_batch_major(xb, n, t)
    y = xb.reshape(n, t, d)
    return y, jnp.stack(new_re), jnp.stack(new_im), jnp.stack(new_ret), jnp.stack(new_pool)


def kernel(x_prompt, x_sample, state_s5_re, state_s5_im, state_ret, state_pool, norm_mix, norm_ffn, norm_final,
           s5_a_re, s5_a_im, s5_log_dt, s5_b_re, s5_b_im, s5_c_re, s5_c_im, s5_d, s5_w_glu, ret_w_in, ret_w_out,
           pool_w, pool_scale, mlp_w_up, mlp_w_down):
    nb = x_prompt.shape[0]
    wts = dict(
        norm_mix=norm_mix, norm_ffn=norm_ffn, norm_final=norm_final, s5_d=s5_d,
        s5_packed=[_s5_weights(s5_a_re[j], s5_a_im[j], s5_log_dt[j], s5_b_re[j], s5_b_im[j], s5_c_re[j], s5_c_im[j])
                   for j in range(s5_a_re.shape[0])],
        s5_w_glu=s5_w_glu.astype(BF16), ret_w_in=ret_w_in.astype(BF16), ret_w_out=ret_w_out.astype(BF16),
        pool_w=pool_w.astype(BF16), pool_scale=pool_scale,
        mlp_w_up=mlp_w_up.astype(BF16), mlp_w_down=mlp_w_down.astype(BF16))
    z_re = jnp.zeros((state_s5_re.shape[0], nb) + state_s5_re.shape[2:], state_s5_re.dtype)
    z_im = jnp.zeros((state_s5_im.shape[0], nb) + state_s5_im.shape[2:], state_s5_im.dtype)
    z_ret = jnp.zeros((state_ret.shape[0], nb) + state_ret.shape[2:], state_ret.dtype)
    z_pool = jnp.zeros((state_pool.shape[0], nb) + state_pool.shape[2:], state_pool.dtype)
    y_p, p_re, p_im, p_ret, p_pool = _trunk(x_prompt, 0, z_re, z_im, z_ret, z_pool, wts)
    y_s, s_re, s_im, s_ret, s_pool = _trunk(x_sample, PAST_LEN, state_s5_re, state_s5_im, state_ret, state_pool, wts)
    return (y_p, y_s, p_re, p_im, p_ret, p_pool, s_re, s_im, s_ret, s_pool)
```

```python
import functools
import math

import jax
import jax.numpy as jnp
from jax import lax
from jax.experimental import pallas as pl
from jax.experimental.pallas import tpu as pltpu

F32 = jnp.float32
BF16 = jnp.bfloat16

EPS = 1e-6
S5_GROUP = 16
S5_STATE = 64
RET_HEADS = 4
RET_CHUNK = 128
ROPE_BASE = 10000.0
POOL_WINDOWS = (2, 4, 8, 16)
POOL_TAIL = 16

SUBLANES = 8
BF16_ROWS = 16
MXU_DIM = 256
ROW_TILE = 512
SCAN_LANES = 512
VMEM_LIMIT = 56 * 1024 * 1024
PAST_LEN = 16384


def _params(n_axes):
    return pltpu.CompilerParams(dimension_semantics=("arbitrary",) * n_axes, vmem_limit_bytes=VMEM_LIMIT)


def _const_spec(shape):
    zeros = (0,) * len(shape)
    return pl.BlockSpec(shape, lambda *_: zeros, pipeline_mode=pl.Buffered(1))


def _layer_spec(stacked_shape, layer):
    idx = (layer,) + (0,) * (len(stacked_shape) - 1)
    return pl.BlockSpec((None,) + tuple(stacked_shape[1:]), lambda *_: idx, pipeline_mode=pl.Buffered(1))


def _can_relayout_in_kernel(n, t):
    steps = ROW_TILE // n
    return ROW_TILE % n == 0 and steps % SUBLANES == 0 and t % steps == 0


def _row_specs(n, t, d, src, dst):
    tm_spec = pl.BlockSpec((ROW_TILE, d), lambda i: (i, 0))
    if src == dst:
        return tm_spec, tm_spec, (n * t, d), (n * t, d)
    bm_spec = pl.BlockSpec((n, ROW_TILE // n, d), lambda i: (0, i, 0))
    if src == 'bm':
        return bm_spec, tm_spec, (n, t, d), (n * t, d)
    return tm_spec, bm_spec, (n * t, d), (n, t, d)


def _rms(x, g):
    ms = jnp.mean(x * x, axis=-1, keepdims=True)
    return x * lax.rsqrt(ms + EPS) * g


def _mlp_body(x_ref, g_ref, wu_ref, wd_ref, gf_ref, o_ref, *, final, ff_chunk):
    x = x_ref[...].reshape(-1, x_ref.shape[-1])
    h = _rms(x, g_ref[...]).astype(BF16)
    acc = jnp.zeros_like(x)
    d_ff = wu_ref.shape[1]
    for c in range(d_ff // ff_chunk):
        cols = slice(c * ff_chunk, (c + 1) * ff_chunk)
        a = jnp.dot(h, wu_ref[:, cols], preferred_element_type=F32)
        a = jnp.square(jnp.maximum(a, 0.0)).astype(BF16)
        acc = acc + jnp.dot(a, wd_ref[cols, :], preferred_element_type=F32)
    y = x + acc
    if final:
        y = _rms(y, gf_ref[...])
    if len(x_ref.shape) == 3 and len(o_ref.shape) == 2:
        y = pltpu.einshape("ntd->tnd", y.reshape(x_ref.shape)).reshape(o_ref.shape)
    elif len(x_ref.shape) == 2 and len(o_ref.shape) == 3:
        n, steps, d = o_ref.shape
        y = pltpu.einshape("tnd->ntd", y.reshape(steps, n, d))
    o_ref[...] = y


def _mlp(x, g, w_up, w_down, layer, g_final, final, n, t, src, dst):
    d = x.shape[1]
    in_spec, out_spec, in_shape, out_shape = _row_specs(n, t, d, src, dst)
    y = pl.pallas_call(
        functools.partial(_mlp_body, final=final, ff_chunk=1024),
        grid=(n * t // ROW_TILE,),
        in_specs=[in_spec, _layer_spec(g.shape, layer), _layer_spec(w_up.shape, layer),
                  _layer_spec(w_down.shape, layer), _const_spec((1, d))],
        out_specs=out_spec,
        out_shape=jax.ShapeDtypeStruct(out_shape, F32),
        compiler_params=_params(1),
        name="mlp_final" if final else "mlp",
    )(x.reshape(in_shape), g, w_up, w_down, g_final.reshape(1, d))
    return y.reshape(n * t, d)


def _mm_res_body(a_ref, w_ref, x_ref, o_ref, *, glu):
    ab = jnp.dot(a_ref[...], w_ref[...], preferred_element_type=F32)
    if glu:
        d = x_ref.shape[1]
        ab = ab[:, :d] * jax.nn.sigmoid(ab[:, d:])
    o_ref[...] = x_ref[...] + ab


def _mm_res(a, w, layer, x, glu):
    rows, d = x.shape
    k = a.shape[1]
    return pl.pallas_call(
        functools.partial(_mm_res_body, glu=glu),
        grid=(rows // ROW_TILE,),
        in_specs=[pl.BlockSpec((ROW_TILE, k), lambda i: (i, 0)), _layer_spec(w.shape, layer),
                  pl.BlockSpec((ROW_TILE, d), lambda i: (i, 0))],
        out_specs=pl.BlockSpec((ROW_TILE, d), lambda i: (i, 0)),
        out_shape=jax.ShapeDtypeStruct((rows, d), F32),
        compiler_params=_params(1),
        name="glu_residual" if glu else "proj_residual",
    )(a, w, x)


def _s5_body(x_ref, g_ref, bw_ref, are_ref, aim_ref, cre_ref, cim_ref, dsk_ref, s0re_ref, s0im_ref,
             gl_ref, sre_out, sim_out, bu_re, bu_im, st_re, st_im, *, nb, tc):
    i = pl.program_id(0)

    @pl.when(i == 0)
    def _():
        st_re[...] = s0re_ref[...]
        st_im[...] = s0im_ref[...]

    h = _rms(x_ref[...], g_ref[...])
    hb = h.astype(BF16)
    n_slab = bw_ref.shape[0]
    n_state_slab = bw_ref.shape[2] // 2
    for j in range(n_slab):
        r = jnp.dot(hb[:, j * MXU_DIM:(j + 1) * MXU_DIM], bw_ref[j], preferred_element_type=F32)
        cols = slice(j * n_state_slab, (j + 1) * n_state_slab)
        bu_re[:, cols] = r[:, :n_state_slab]
        bu_im[:, cols] = r[:, n_state_slab:]

    n_lanes = bu_re.shape[1]
    for c in range(n_lanes // SCAN_LANES):
        lanes = slice(c * SCAN_LANES, (c + 1) * SCAN_LANES)
        ar = jnp.broadcast_to(are_ref[:, lanes], (SUBLANES, SCAN_LANES))
        ai = jnp.broadcast_to(aim_ref[:, lanes], (SUBLANES, SCAN_LANES))

        def row_group(rg, carry, lanes=lanes, ar=ar, ai=ai):
            g0 = pl.multiple_of(rg * SUBLANES, SUBLANES)
            sr = st_re[pl.ds(g0, SUBLANES), lanes]
            si = st_im[pl.ds(g0, SUBLANES), lanes]

            def step(t, s):
                sr, si = s
                r0 = pl.multiple_of(t * nb + g0, SUBLANES)
                nr = ar * sr - ai * si + bu_re[pl.ds(r0, SUBLANES), lanes]
                ni = ar * si + ai * sr + bu_im[pl.ds(r0, SUBLANES), lanes]
                bu_re[pl.ds(r0, SUBLANES), lanes] = nr
                bu_im[pl.ds(r0, SUBLANES), lanes] = ni
                return nr, ni

            sr, si = lax.fori_loop(0, tc, step, (sr, si), unroll=min(tc, 8))
            st_re[pl.ds(g0, SUBLANES), lanes] = sr
            st_im[pl.ds(g0, SUBLANES), lanes] = si
            return carry

        lax.fori_loop(0, nb // SUBLANES, row_group, 0)

    k_slab = cre_ref.shape[1]
    ys = []
    for j in range(cre_ref.shape[0]):
        cols = slice(j * k_slab, (j + 1) * k_slab)
        y = jnp.dot(bu_re[:, cols].astype(BF16), cre_ref[j], preferred_element_type=F32)
        y = y + jnp.dot(bu_im[:, cols].astype(BF16), cim_ref[j], preferred_element_type=F32)
        ys.append(y)
    y = jnp.concatenate(ys, axis=1) + dsk_ref[...] * h
    gl_ref[...] = jax.nn.gelu(y, approximate=True).astype(BF16)

    @pl.when(i == pl.num_programs(0) - 1)
    def _():
        sre_out[...] = st_re[...]
        sim_out[...] = st_im[...]


def _s5_core(xt, g, layer, bw, a_re, a_im, c_re, c_im, d_skip, s0_re, s0_im, nb, tc):
    rows, d = xt.shape
    n_lanes = a_re.shape[1]
    r = tc * nb
    return pl.pallas_call(
        functools.partial(_s5_body, nb=nb, tc=tc),
        grid=(rows // r,),
        in_specs=[pl.BlockSpec((r, d), lambda i: (i, 0)),
                  _layer_spec(g.shape, layer), _const_spec(bw.shape), _const_spec((1, n_lanes)), _const_spec((1, n_lanes)),
                  _const_spec(c_re.shape), _const_spec(c_im.shape), _const_spec((1, d)),
                  _const_spec((nb, n_lanes)), _const_spec((nb, n_lanes))],
        out_specs=[pl.BlockSpec((r, d), lambda i: (i, 0)),
                   pl.BlockSpec((nb, n_lanes), lambda i: (0, 0)), pl.BlockSpec((nb, n_lanes), lambda i: (0, 0))],
        out_shape=[jax.ShapeDtypeStruct((rows, d), BF16),
                   jax.ShapeDtypeStruct((nb, n_lanes), F32), jax.ShapeDtypeStruct((nb, n_lanes), F32)],
        scratch_shapes=[pltpu.VMEM((r, n_lanes), F32), pltpu.VMEM((r, n_lanes), F32),
                        pltpu.VMEM((nb, n_lanes), F32), pltpu.VMEM((nb, n_lanes), F32)],
        compiler_params=_params(1),
        name="s5_core",
    )(xt, g, bw, a_re, a_im, c_re, c_im, d_skip.reshape(1, d), s0_re, s0_im)


def _s5_weights(a_re, a_im, log_dt, b_re, b_im, c_re, c_im):
    groups, p, hch = b_re.shape
    dt = jnp.exp(log_dt.astype(F32))[:, None]
    lr, li = a_re.astype(F32), a_im.astype(F32)
    mag = jnp.exp(lr * dt)
    ab_re, ab_im = mag * jnp.cos(li * dt), mag * jnp.sin(li * dt)
    den = lr * lr + li * li
    nr, ni = ab_re - 1.0, ab_im
    f_re = (nr * lr + ni * li) / den
    f_im = (ni * lr - nr * li) / den
    br, bi = b_re.astype(F32), b_im.astype(F32)
    bb_re = f_re[..., None] * br - f_im[..., None] * bi
    bb_im = f_re[..., None] * bi + f_im[..., None] * br
    gs = MXU_DIM // hch
    n_slab = groups // gs

    def block_diag(per_group):
        r, c = per_group.shape[1:]
        tiled = jnp.tile(per_group.reshape(n_slab, gs * r, c), (1, 1, gs))
        row_g = lax.broadcasted_iota(jnp.int32, (gs * r, gs * c), 0) // r
        col_g = lax.broadcasted_iota(jnp.int32, (gs * r, gs * c), 1) // c
        return jnp.where(row_g == col_g, tiled, 0.0).astype(BF16)

    to_hp = lambda bb: bb.transpose(0, 2, 1)
    bw = jnp.concatenate([block_diag(to_hp(bb_re)), block_diag(to_hp(bb_im))], axis=2)
    cw_re = block_diag(c_re.astype(F32).transpose(0, 2, 1))
    cw_im = block_diag(-c_im.astype(F32).transpose(0, 2, 1))
    return bw, ab_re.reshape(1, groups * p), ab_im.reshape(1, groups * p), cw_re, cw_im


def _ret_in_body(x_ref, g_ref, w_ref, cos_ref, sin_ref, q_ref, k_ref, v_ref, gate_ref, *, d, qk):
    h = _rms(x_ref[...], g_ref[...]).astype(BF16)
    cos, sin = cos_ref[...], sin_ref[...]
    half = qk // 2

    def rot(z):
        outs = []
        for hd in range(d // qk):
            z1 = z[:, hd * qk:hd * qk + half]
            z2 = z[:, hd * qk + half:(hd + 1) * qk]
            outs += [z1 * cos - z2 * sin, z1 * sin + z2 * cos]
        return jnp.concatenate(outs, axis=1)

    q = jnp.dot(h, w_ref[:, 0:d], preferred_element_type=F32)
    q_ref[...] = (rot(q) * (qk ** -0.5)).astype(BF16)
    k = jnp.dot(h, w_ref[:, d:2 * d], preferred_element_type=F32)
    k_ref[...] = rot(k).astype(BF16)
    v_ref[...] = jnp.dot(h, w_ref[:, 2 * d:4 * d], preferred_element_type=F32).astype(BF16)
    gate_ref[...] = jnp.dot(h, w_ref[:, 4 * d:6 * d], preferred_element_type=F32).astype(BF16)


def _ret_in(x, g, layer, w_in, j, cos_tab, sin_tab):
    rows, d = x.shape
    qk = d // RET_HEADS
    tab_blocks = cos_tab.shape[0] // ROW_TILE
    row_spec = lambda width: pl.BlockSpec((ROW_TILE, width), lambda i: (i, 0))
    tab_spec = pl.BlockSpec((ROW_TILE, qk // 2), lambda i: (i % tab_blocks, 0))
    return pl.pallas_call(
        functools.partial(_ret_in_body, d=d, qk=qk),
        grid=(rows // ROW_TILE,),
        in_specs=[row_spec(d), _layer_spec(g.shape, layer), _layer_spec(w_in.shape, j), tab_spec, tab_spec],
        out_specs=[row_spec(d), row_spec(d), row_spec(2 * d), row_spec(2 * d)],
        out_shape=[jax.ShapeDtypeStruct((rows, d), BF16), jax.ShapeDtypeStruct((rows, d), BF16),
                   jax.ShapeDtypeStruct((rows, 2 * d), BF16), jax.ShapeDtypeStruct((rows, 2 * d), BF16)],
        compiler_params=_params(1),
        name="ret_in",
    )(x, g, w_in, cos_tab, sin_tab)


def _ret_core_body(q_ref, k_ref, v_ref, gate_ref, s0_ref, dmask_ref, qdec_ref, kdec_ref, cdec_ref,
                   og_ref, s_ref, *, lb, lp, qk, vd):
    c = pl.program_id(1)

    @pl.when(c == 0)
    def _():
        s_ref[...] = s0_ref[...]

    def padded(z):
        if lb == lp:
            return z
        return jnp.concatenate([z, jnp.zeros((lp - lb, z.shape[1]), z.dtype)], axis=0)

    q, k, v = padded(q_ref[...]), padded(k_ref[...]), padded(v_ref[...])
    gate = padded(gate_ref[...])
    for hd in range(RET_HEADS):
        qh = q[:, hd * qk:(hd + 1) * qk]
        kh = k[:, hd * qk:(hd + 1) * qk]
        kf = kh.astype(F32)
        vh = v[:, hd * vd:(hd + 1) * vd]
        s_old = s_ref[hd]
        scores = lax.dot_general(qh, kh, (((1,), (1,)), ((), ())), preferred_element_type=F32) * dmask_ref[hd]
        o = jnp.dot(scores.astype(BF16), vh, preferred_element_type=F32)
        o = o + qdec_ref[hd] * jnp.dot(qh, s_old.astype(BF16), preferred_element_type=F32)
        kd = (kf * kdec_ref[hd]).astype(BF16)
        s_ref[hd] = cdec_ref[hd] * s_old + lax.dot_general(kd, vh, (((0,), (0,)), ((), ())),
                                                           preferred_element_type=F32)
        mu = jnp.mean(o, axis=-1, keepdims=True)
        oc = o - mu
        o = oc * lax.rsqrt(jnp.mean(oc * oc, axis=-1, keepdims=True) + EPS)
        og = o * jax.nn.silu(gate[:, hd * vd:(hd + 1) * vd].astype(F32))
        og_ref[:, hd * vd:(hd + 1) * vd] = og[:lb].astype(BF16)


def _ret_core(q, k, v, gate, s0, consts, n, nc, lb):
    rows, d = q.shape
    qk, vd = d // RET_HEADS, 2 * d // RET_HEADS
    dmask, qdec, kdec, cdec = consts
    lp = dmask.shape[1]
    row_spec = lambda width: pl.BlockSpec((lb, width), lambda b, c: (b * nc + c, 0))
    s_spec = pl.BlockSpec((None, RET_HEADS, qk, vd), lambda b, c: (b, 0, 0, 0))
    return pl.pallas_call(
        functools.partial(_ret_core_body, lb=lb, lp=lp, qk=qk, vd=vd),
        grid=(n, nc),
        in_specs=[row_spec(d), row_spec(d), row_spec(2 * d), row_spec(2 * d), s_spec,
                  _const_spec(dmask.shape), _const_spec(qdec.shape), _const_spec(kdec.shape), _const_spec(cdec.shape)],
        out_specs=[row_spec(2 * d), s_spec],
        out_shape=[jax.ShapeDtypeStruct((rows, 2 * d), BF16), jax.ShapeDtypeStruct(s0.shape, F32)],
        compiler_params=_params(2),
        name="ret_core",
    )(q, k, v, gate, s0, dmask, qdec, kdec, cdec)


def _ret_consts(l_real, l_pad, qk, vd):
    log_g = jnp.log1p(-jnp.exp2(-5.0 - jnp.arange(RET_HEADS, dtype=F32)))
    idx = jnp.arange(l_real, dtype=F32)
    diff = idx[:, None] - idx[None, :]
    dmask = jnp.exp(jnp.maximum(diff, 0.0)[None] * log_g[:, None, None]) * (diff >= 0)[None]
    q_dec = jnp.exp((idx + 1.0)[:, None] * log_g[None, :])
    k_dec = jnp.exp((l_real - 1.0 - idx)[:, None] * log_g[None, :])
    chunk_dec = jnp.exp(l_real * log_g)
    pad = l_pad - l_real
    dmask = jnp.pad(dmask, ((0, 0), (0, pad), (0, pad)))
    q_dec = jnp.pad(q_dec.T, ((0, 0), (0, pad)))
    k_dec = jnp.pad(k_dec.T, ((0, 0), (0, pad)))
    qdec = jnp.broadcast_to(q_dec[:, :, None], (RET_HEADS, l_pad, vd))
    kdec = jnp.broadcast_to(k_dec[:, :, None], (RET_HEADS, l_pad, qk))
    cdec = jnp.broadcast_to(chunk_dec[:, None, None], (RET_HEADS, 1, vd))
    return dmask, qdec, kdec, cdec


def _rope_tables(pos0, t, qk):
    half = qk // 2
    inv = ROPE_BASE ** (-jnp.arange(half, dtype=F32) / half)
    pos = pos0 + jnp.arange(t)
    ang = pos.astype(F32)[:, None] * inv[None, :]
    return jnp.cos(ang), jnp.sin(ang)


def _pool_body(x_ref, g_ref, buf_ref, w_ref, sc_ref, o_ref, tail_out, tail, *, nb, tc, pos0):
    i = pl.program_id(0)

    @pl.when(i == 0)
    def _():
        tail[...] = buf_ref[...]

    x = x_ref[...]
    h = _rms(x, g_ref[...])
    ext = jnp.concatenate([tail[...], h], axis=0)
    rows = tc * nb
    grp = x.shape[1] // len(POOL_WINDOWS)
    t_idx = i * tc + lax.shift_right_logical(lax.broadcasted_iota(jnp.int32, (rows, grp), 0), nb.bit_length() - 1)
    for gi, w in enumerate(POOL_WINDOWS):
        cols = slice(gi * grp, (gi + 1) * grp)
        s, first, span = ext[:, cols], 0, 1
        while span < w:
            s = s[span * nb:] + s[:-span * nb]
            first += span
            span *= 2
        s = s[(POOL_TAIL - first) * nb:]
        cnt = jnp.minimum(w, pos0 + t_idx + 1).astype(F32)
        pooled = s / cnt - h[:, cols]
        z = jnp.dot(pooled.astype(BF16), w_ref[gi], preferred_element_type=F32)
        o_ref[:, cols] = x[:, cols] + z * sc_ref[:, cols]
    new_tail = ext[rows:]
    tail[...] = new_tail

    @pl.when(i == pl.num_programs(0) - 1)
    def _():
        tail_out[...] = new_tail


def _pool(xt, g, layer, buf_t, w, scale, j, nb, tc, pos0):
    rows, d = xt.shape
    assert nb & (nb - 1) == 0
    r = tc * nb
    tail_rows = POOL_TAIL * nb
    return pl.pallas_call(
        functools.partial(_pool_body, nb=nb, tc=tc, pos0=pos0),
        grid=(rows // r,),
        in_specs=[pl.BlockSpec((r, d), lambda i: (i, 0)), _layer_spec(g.shape, layer), _const_spec((tail_rows, d)),
                  _layer_spec(w.shape, j), _layer_spec(scale.shape, j)],
        out_specs=[pl.BlockSpec((r, d), lambda i: (i, 0)), pl.BlockSpec((tail_rows, d), lambda i: (0, 0))],
        out_shape=[jax.ShapeDtypeStruct((rows, d), F32), jax.ShapeDtypeStruct((tail_rows, d), F32)],
        scratch_shapes=[pltpu.VMEM((tail_rows, d), F32)],
        compiler_params=_params(1),
        name="pool_mixer",
    )(xt, g, buf_t, w, scale)


def _relayout(x, n, t, src, dst):
    if src == dst:
        return x
    lead = (n, t) if src == 'bm' else (t, n)
    return x.reshape(lead + (-1,)).transpose(1, 0, 2).reshape(n * t, -1)


def _mixer_layout(layer):
    return 'bm' if layer % 3 == 1 else 'tm'


def _trunk(x, pos0, st_re, st_im, st_ret, st_pool, wts):
    n, t, d = x.shape
    rows = n * t
    depth = wts['norm_mix'].shape[0]
    qk, vd = d // RET_HEADS, 2 * d // RET_HEADS
    tc = min(max(ROW_TILE // n, 1), t)
    mlp_can_relayout = _can_relayout_in_kernel(n, t)
    new_re, new_im, new_ret, new_pool = [], [], [], []

    xb, layout = x.reshape(rows, d), 'bm'
    for layer in range(depth):
        kind, j = layer % 3, layer // 3
        xb, layout = _relayout(xb, n, t, layout, _mixer_layout(layer)), _mixer_layout(layer)
        if kind == 1:
            l_real = math.gcd(t, RET_CHUNK)
            nc = t // l_real
            lb = max(l_real, BF16_ROWS)
            cos, sin = _rope_tables(pos0, t, qk)
            if t < ROW_TILE:
                cos, sin = jnp.tile(cos, (ROW_TILE // t, 1)), jnp.tile(sin, (ROW_TILE // t, 1))
            q, k, v, gate = _ret_in(xb, wts['norm_mix'], layer, wts['ret_w_in'], j, cos, sin)
            if lb != l_real:
                pad = lambda z: jnp.pad(z.reshape(n * nc, l_real, -1), ((0, 0), (0, lb - l_real), (0, 0))
                                        ).reshape(n * nc * lb, -1)
                q, k, v, gate = pad(q), pad(k), pad(v), pad(gate)
            consts = _ret_consts(l_real, RET_CHUNK, qk, vd)
            og, s_new = _ret_core(q, k, v, gate, st_ret[j], consts, n, nc, lb)
            if lb != l_real:
                og = og.reshape(n * nc, lb, -1)[:, :l_real].reshape(rows, -1)
            xb = _mm_res(og, wts['ret_w_out'], j, xb, glu=False)
            new_ret.append(s_new)
        elif kind == 0:
            bw, a_re, a_im, cw_re, cw_im = wts['s5_packed'][j]
            gl, s_re, s_im = _s5_core(xb, wts['norm_mix'], layer, bw, a_re, a_im, cw_re, cw_im,
                                      wts['s5_d'][j], st_re[j].reshape(n, -1), st_im[j].reshape(n, -1), n, tc)
            xb = _mm_res(gl, wts['s5_w_glu'], j, xb, glu=True)
            new_re.append(s_re.reshape(st_re[j].shape))
            new_im.append(s_im.reshape(st_im[j].shape))
        else:
            buf = st_pool[j]
            buf_t = jnp.pad(buf.transpose(1, 0, 2), ((1, 0), (0, 0), (0, 0))).reshape(POOL_TAIL * n, d)
            xb, tail = _pool(xb, wts['norm_mix'], layer, buf_t, wts['pool_w'], wts['pool_scale'], j, n, tc, pos0)
            new_pool.append(tail.reshape(POOL_TAIL, n, d)[1:].transpose(1, 0, 2))
        final = layer == depth - 1
        wanted = 'bm' if final else _mixer_layout(layer + 1)
        dst = wanted if mlp_can_relayout else layout
        xb = _mlp(xb, wts['norm_ffn'], wts['mlp_w_up'], wts['mlp_w_down'], layer, wts['norm_final'], final,
                  n, t, layout, dst)
        layout = dst
    y = _relayout(xb, n, t, layout, 'bm').reshape(n, t, d)
    return y, jnp.stack(new_re), jnp.stack(new_im), jnp.stack(new_ret), jnp.stack(new_pool)


def kernel(x_prompt, x_sample, state_s5_re, state_s5_im, state_ret, state_pool, norm_mix, norm_ffn, norm_final,
           s5_a_re, s5_a_im, s5_log_dt, s5_b_re, s5_b_im, s5_c_re, s5_c_im, s5_d, s5_w_glu, ret_w_in, ret_w_out,
           pool_w, pool_scale, mlp_w_up, mlp_w_down):
    nb = x_prompt.shape[0]
    depth, d = norm_mix.shape
    wts = dict(
        norm_mix=norm_mix.reshape(depth, 1, d), norm_ffn=norm_ffn.reshape(depth, 1, d), norm_final=norm_final,
        s5_d=s5_d,
        s5_packed=[_s5_weights(s5_a_re[j], s5_a_im[j], s5_log_dt[j], s5_b_re[j], s5_b_im[j], s5_c_re[j], s5_c_im[j])
                   for j in range(s5_a_re.shape[0])],
        s5_w_glu=s5_w_glu.astype(BF16), ret_w_in=ret_w_in.astype(BF16), ret_w_out=ret_w_out.astype(BF16),
        pool_w=pool_w.astype(BF16), pool_scale=pool_scale.reshape(pool_scale.shape[0], 1, d),
        mlp_w_up=mlp_w_up.astype(BF16), mlp_w_down=mlp_w_down.astype(BF16))
    z_re = jnp.zeros((state_s5_re.shape[0], nb) + state_s5_re.shape[2:], state_s5_re.dtype)
    z_im = jnp.zeros((state_s5_im.shape[0], nb) + state_s5_im.shape[2:], state_s5_im.dtype)
    z_ret = jnp.zeros((state_ret.shape[0], nb) + state_ret.shape[2:], state_ret.dtype)
    z_pool = jnp.zeros((state_pool.shape[0], nb) + state_pool.shape[2:], state_pool.dtype)
    y_p, p_re, p_im, p_ret, p_pool = _trunk(x_prompt, 0, z_re, z_im, z_ret, z_pool, wts)
    y_s, s_re, s_im, s_ret, s_pool = _trunk(x_sample, PAST_LEN, state_s5_re, state_s5_im, state_ret, state_pool, wts)
    return (y_p, y_s, p_re, p_im, p_ret, p_pool, s_re, s_im, s_ret, s_pool)
```

```python
import functools
import math

import jax
import jax.numpy as jnp
from jax import lax
from jax.experimental import pallas as pl
from jax.experimental.pallas import tpu as pltpu

F32 = jnp.float32
BF16 = jnp.bfloat16

EPS = 1e-6
RET_HEADS = 4
RET_CHUNK = 128
ROPE_BASE = 10000.0
POOL_WINDOWS = (2, 4, 8, 16)
POOL_TAIL = 16

SUBLANES = 8
BF16_ROWS = 16
MXU_DIM = 256
ROW_TILE = 512
SCAN_LANES = 512
VMEM_LIMIT = 56 * 1024 * 1024
PAST_LEN = 16384


def _params(n_axes):
    return pltpu.CompilerParams(dimension_semantics=("arbitrary",) * n_axes, vmem_limit_bytes=VMEM_LIMIT)


def _const_spec(shape):
    zeros = (0,) * len(shape)
    return pl.BlockSpec(shape, lambda *_: zeros, pipeline_mode=pl.Buffered(1))


def _layer_spec(stacked_shape, layer):
    idx = (layer,) + (0,) * (len(stacked_shape) - 1)
    return pl.BlockSpec((None,) + tuple(stacked_shape[1:]), lambda *_: idx, pipeline_mode=pl.Buffered(1))


def _can_relayout_in_kernel(n, t):
    steps = ROW_TILE // n
    return ROW_TILE % n == 0 and steps % BF16_ROWS == 0 and t % steps == 0


def _tile_spec(layout, mixed, n, width):
    if mixed and layout == 'bm':
        return pl.BlockSpec((n, ROW_TILE // n, width), lambda i: (0, i, 0))
    return pl.BlockSpec((ROW_TILE, width), lambda i: (i, 0))


def _tile_view(x, layout, mixed, n, t):
    return x.reshape(n, t, x.shape[-1]) if mixed and layout == 'bm' else x


def _permute_rows(y, n, to):
    rows, d = y.shape
    lead = (rows // n, n) if to == 'bm' else (n, rows // n)
    return jnp.swapaxes(y.reshape(lead + (d,)), 0, 1).reshape(rows, d)


def _rms(x, g):
    ms = jnp.mean(x * x, axis=-1, keepdims=True)
    return x * lax.rsqrt(ms + EPS) * g


def _mlp_body(*refs, proj, final, ff_chunk, n, mix_to, out_to):
    if proj is None:
        x_ref, g_ref, wu_ref, wd_ref, gf_ref, o_ref = refs
    else:
        a_ref, wp_ref, x_ref, g_ref, wu_ref, wd_ref, gf_ref, o_ref = refs
    d = x_ref.shape[-1]
    x = x_ref[...].reshape(ROW_TILE, d)
    if proj is not None:
        mix = jnp.dot(a_ref[...].reshape(ROW_TILE, a_ref.shape[-1]), wp_ref[...], preferred_element_type=F32)
        if proj == 'glu':
            mix = mix[:, :d] * jax.nn.sigmoid(mix[:, d:])
        if mix_to is not None:
            mix = _permute_rows(mix, n, mix_to)
        x = x + mix
    h = _rms(x, g_ref[...]).astype(BF16)
    acc = jnp.zeros_like(x)
    d_ff = wu_ref.shape[1]
    for c in range(d_ff // ff_chunk):
        cols = slice(c * ff_chunk, (c + 1) * ff_chunk)
        a = jnp.dot(h, wu_ref[:, cols], preferred_element_type=F32)
        a = jnp.square(jnp.maximum(a, 0.0)).astype(BF16)
        acc = acc + jnp.dot(a, wd_ref[cols, :], preferred_element_type=F32)
    y = x + acc
    if final:
        y = _rms(y, gf_ref[...])
    if out_to is not None:
        y = _permute_rows(y, n, out_to)
    o_ref[...] = y.reshape(o_ref.shape)


def _mlp(a, a_layout, w_proj, j, proj, x, x_layout, g, w_up, w_down, layer, g_final, final, n, t, dst):
    d = x.shape[1]
    layouts = {x_layout, dst} | ({a_layout} if proj is not None else set())
    mixed = len(layouts) > 1
    assert not mixed or _can_relayout_in_kernel(n, t)
    operands, specs = [], []
    if proj is not None:
        operands += [_tile_view(a, a_layout, mixed, n, t), w_proj]
        specs += [_tile_spec(a_layout, mixed, n, a.shape[1]), _layer_spec(w_proj.shape, j)]
    operands += [_tile_view(x, x_layout, mixed, n, t), g, w_up, w_down, g_final.reshape(1, d)]
    specs += [_tile_spec(x_layout, mixed, n, d), _layer_spec(g.shape, layer), _layer_spec(w_up.shape, layer),
              _layer_spec(w_down.shape, layer), _const_spec((1, d))]
    out_shape = (n, t, d) if mixed and dst == 'bm' else (n * t, d)
    y = pl.pallas_call(
        functools.partial(_mlp_body, proj=proj, final=final, ff_chunk=1024, n=n,
                          mix_to=x_layout if proj is not None and a_layout != x_layout else None,
                          out_to=dst if dst != x_layout else None),
        grid=(n * t // ROW_TILE,),
        in_specs=specs,
        out_specs=_tile_spec(dst, mixed, n, d),
        out_shape=jax.ShapeDtypeStruct(out_shape, F32),
        compiler_params=_params(1),
        name=("mlp" if proj is None else proj + "_mlp") + ("_final" if final else ""),
    )(*operands)
    return y.reshape(n * t, d)


def _s5_body(x_ref, g_ref, bw_ref, are_ref, aim_ref, cre_ref, cim_ref, dsk_ref, s0re_ref, s0im_ref,
             gl_ref, sre_out, sim_out, bu_re, bu_im, st_re, st_im, *, nb, tc):
    i = pl.program_id(0)

    @pl.when(i == 0)
    def _():
        st_re[...] = s0re_ref[...]
        st_im[...] = s0im_ref[...]

    d = x_ref.shape[-1]
    x = x_ref[...].reshape(tc * nb, d)
    if len(x_ref.shape) == 3:
        x = _permute_rows(x, nb, 'tm')
    h = _rms(x, g_ref[...])
    hb = h.astype(BF16)
    n_state_slab = bw_ref.shape[2] // 2
    ys = []
    for j in range(bw_ref.shape[0]):
        r = jnp.dot(hb[:, j * MXU_DIM:(j + 1) * MXU_DIM], bw_ref[j], preferred_element_type=F32)
        cols = slice(j * n_state_slab, (j + 1) * n_state_slab)
        bu_re[:, cols] = r[:, :n_state_slab]
        bu_im[:, cols] = r[:, n_state_slab:]
        for c in range(n_state_slab // SCAN_LANES):
            lanes = slice(j * n_state_slab + c * SCAN_LANES, j * n_state_slab + (c + 1) * SCAN_LANES)
            ar = jnp.broadcast_to(are_ref[:, lanes], (SUBLANES, SCAN_LANES))
            ai = jnp.broadcast_to(aim_ref[:, lanes], (SUBLANES, SCAN_LANES))
            for rg in range(nb // SUBLANES):
                seqs = slice(rg * SUBLANES, (rg + 1) * SUBLANES)
                sr, si = st_re[seqs, lanes], st_im[seqs, lanes]
                for t in range(tc):
                    rows = slice(t * nb + rg * SUBLANES, t * nb + (rg + 1) * SUBLANES)
                    sr, si = (ar * sr - ai * si + bu_re[rows, lanes], ar * si + ai * sr + bu_im[rows, lanes])
                    bu_re[rows, lanes] = sr
                    bu_im[rows, lanes] = si
                st_re[seqs, lanes] = sr
                st_im[seqs, lanes] = si
        y = jnp.dot(bu_re[:, cols].astype(BF16), cre_ref[j], preferred_element_type=F32)
        ys.append(y + jnp.dot(bu_im[:, cols].astype(BF16), cim_ref[j], preferred_element_type=F32))
    y = jnp.concatenate(ys, axis=1) + dsk_ref[...] * h
    gl_ref[...] = jax.nn.gelu(y, approximate=True).astype(BF16)

    @pl.when(i == pl.num_programs(0) - 1)
    def _():
        sre_out[...] = st_re[...]
        sim_out[...] = st_im[...]


def _s5_core(x, x_layout, g, layer, bw, a_re, a_im, c_re, c_im, d_skip, s0_re, s0_im, nb, t, tc):
    rows, d = x.shape
    n_lanes = a_re.shape[1]
    r = tc * nb
    from_bm = x_layout == 'bm'
    assert not from_bm or (r == ROW_TILE and _can_relayout_in_kernel(nb, t))
    return pl.pallas_call(
        functools.partial(_s5_body, nb=nb, tc=tc),
        grid=(rows // r,),
        in_specs=[_tile_spec(x_layout, from_bm, nb, d) if from_bm else pl.BlockSpec((r, d), lambda i: (i, 0)),
                  _layer_spec(g.shape, layer), _const_spec(bw.shape), _const_spec((1, n_lanes)), _const_spec((1, n_lanes)),
                  _const_spec(c_re.shape), _const_spec(c_im.shape), _const_spec((1, d)),
                  _const_spec((nb, n_lanes)), _const_spec((nb, n_lanes))],
        out_specs=[pl.BlockSpec((r, d), lambda i: (i, 0)),
                   pl.BlockSpec((nb, n_lanes), lambda i: (0, 0)), pl.BlockSpec((nb, n_lanes), lambda i: (0, 0))],
        out_shape=[jax.ShapeDtypeStruct((rows, d), BF16),
                   jax.ShapeDtypeStruct((nb, n_lanes), F32), jax.ShapeDtypeStruct((nb, n_lanes), F32)],
        scratch_shapes=[pltpu.VMEM((r, n_lanes), F32), pltpu.VMEM((r, n_lanes), F32),
                        pltpu.VMEM((nb, n_lanes), F32), pltpu.VMEM((nb, n_lanes), F32)],
        compiler_params=_params(1),
        name="s5_core",
    )(_tile_view(x, x_layout, from_bm, nb, t), g, bw, a_re, a_im, c_re, c_im, d_skip.reshape(1, d), s0_re, s0_im)


def _s5_weights(a_re, a_im, log_dt, b_re, b_im, c_re, c_im):
    groups, p, hch = b_re.shape
    dt = jnp.exp(log_dt.astype(F32))[:, None]
    lr, li = a_re.astype(F32), a_im.astype(F32)
    mag = jnp.exp(lr * dt)
    ab_re, ab_im = mag * jnp.cos(li * dt), mag * jnp.sin(li * dt)
    den = lr * lr + li * li
    nr, ni = ab_re - 1.0, ab_im
    f_re = (nr * lr + ni * li) / den
    f_im = (ni * lr - nr * li) / den
    br, bi = b_re.astype(F32), b_im.astype(F32)
    bb_re = f_re[..., None] * br - f_im[..., None] * bi
    bb_im = f_re[..., None] * bi + f_im[..., None] * br
    gs = MXU_DIM // hch
    n_slab = groups // gs

    def block_diag(per_group):
        r, c = per_group.shape[1:]
        tiled = jnp.tile(per_group.reshape(n_slab, gs * r, c), (1, 1, gs))
        row_g = lax.broadcasted_iota(jnp.int32, (gs * r, gs * c), 0) // r
        col_g = lax.broadcasted_iota(jnp.int32, (gs * r, gs * c), 1) // c
        return jnp.where(row_g == col_g, tiled, 0.0).astype(BF16)

    to_hp = lambda bb: bb.transpose(0, 2, 1)
    bw = jnp.concatenate([block_diag(to_hp(bb_re)), block_diag(to_hp(bb_im))], axis=2)
    cw_re = block_diag(c_re.astype(F32).transpose(0, 2, 1))
    cw_im = block_diag(-c_im.astype(F32).transpose(0, 2, 1))
    return bw, ab_re.reshape(1, groups * p), ab_im.reshape(1, groups * p), cw_re, cw_im


def _ret_in_body(x_ref, g_ref, w_ref, cos_ref, sin_ref, q_ref, k_ref, v_ref, gate_ref, *, d, qk):
    h = _rms(x_ref[...], g_ref[...]).astype(BF16)
    cos, sin = cos_ref[...], sin_ref[...]
    half = qk // 2

    def rot(z):
        outs = []
        for hd in range(d // qk):
            z1 = z[:, hd * qk:hd * qk + half]
            z2 = z[:, hd * qk + half:(hd + 1) * qk]
            outs += [z1 * cos - z2 * sin, z1 * sin + z2 * cos]
        return jnp.concatenate(outs, axis=1)

    q = jnp.dot(h, w_ref[:, 0:d], preferred_element_type=F32)
    q_ref[...] = (rot(q) * (qk ** -0.5)).astype(BF16)
    k = jnp.dot(h, w_ref[:, d:2 * d], preferred_element_type=F32)
    k_ref[...] = rot(k).astype(BF16)
    v_ref[...] = jnp.dot(h, w_ref[:, 2 * d:4 * d], preferred_element_type=F32).astype(BF16)
    gate_ref[...] = jnp.dot(h, w_ref[:, 4 * d:6 * d], preferred_element_type=F32).astype(BF16)


def _ret_in(x, g, layer, w_in, j, cos_tab, sin_tab):
    rows, d = x.shape
    qk = d // RET_HEADS
    tab_blocks = cos_tab.shape[0] // ROW_TILE
    row_spec = lambda width: pl.BlockSpec((ROW_TILE, width), lambda i: (i, 0))
    tab_spec = pl.BlockSpec((ROW_TILE, qk // 2), lambda i: (i % tab_blocks, 0))
    return pl.pallas_call(
        functools.partial(_ret_in_body, d=d, qk=qk),
        grid=(rows // ROW_TILE,),
        in_specs=[row_spec(d), _layer_spec(g.shape, layer), _layer_spec(w_in.shape, j), tab_spec, tab_spec],
        out_specs=[row_spec(d), row_spec(d), row_spec(2 * d), row_spec(2 * d)],
        out_shape=[jax.ShapeDtypeStruct((rows, d), BF16), jax.ShapeDtypeStruct((rows, d), BF16),
                   jax.ShapeDtypeStruct((rows, 2 * d), BF16), jax.ShapeDtypeStruct((rows, 2 * d), BF16)],
        compiler_params=_params(1),
        name="ret_in",
    )(x, g, w_in, cos_tab, sin_tab)


def _ret_core_body(q_ref, k_ref, v_ref, gate_ref, s0_ref, dmask_ref, qdec_ref, kdec_ref, cdec_ref,
                   og_ref, s_ref, *, lb, lp, qk, vd):
    c = pl.program_id(1)

    @pl.when(c == 0)
    def _():
        s_ref[...] = s0_ref[...]

    def padded(z):
        if lb == lp:
            return z
        return jnp.concatenate([z, jnp.zeros((lp - lb, z.shape[1]), z.dtype)], axis=0)

    q, k, v = padded(q_ref[...]), padded(k_ref[...]), padded(v_ref[...])
    gate = padded(gate_ref[...])
    for hd in range(RET_HEADS):
        qh = q[:, hd * qk:(hd + 1) * qk]
        kh = k[:, hd * qk:(hd + 1) * qk]
        kf = kh.astype(F32)
        vh = v[:, hd * vd:(hd + 1) * vd]
        s_old = s_ref[hd]
        scores = lax.dot_general(qh, kh, (((1,), (1,)), ((), ())), preferred_element_type=F32) * dmask_ref[hd]
        o = jnp.dot(scores.astype(BF16), vh, preferred_element_type=F32)
        o = o + qdec_ref[hd] * jnp.dot(qh, s_old.astype(BF16), preferred_element_type=F32)
        kd = (kf * kdec_ref[hd]).astype(BF16)
        s_ref[hd] = cdec_ref[hd] * s_old + lax.dot_general(kd, vh, (((0,), (0,)), ((), ())),
                                                           preferred_element_type=F32)
        mu = jnp.mean(o, axis=-1, keepdims=True)
        oc = o - mu
        o = oc * lax.rsqrt(jnp.mean(oc * oc, axis=-1, keepdims=True) + EPS)
        og = o * jax.nn.silu(gate[:, hd * vd:(hd + 1) * vd].astype(F32))
        og_ref[:, hd * vd:(hd + 1) * vd] = og[:lb].astype(BF16)


def _ret_core(q, k, v, gate, s0, consts, n, nc, lb):
    rows, d = q.shape
    qk, vd = d // RET_HEADS, 2 * d // RET_HEADS
    dmask, qdec, kdec, cdec = consts
    lp = dmask.shape[1]
    row_spec = lambda width: pl.BlockSpec((lb, width), lambda b, c: (b * nc + c, 0))
    s_spec = pl.BlockSpec((None, RET_HEADS, qk, vd), lambda b, c: (b, 0, 0, 0))
    return pl.pallas_call(
        functools.partial(_ret_core_body, lb=lb, lp=lp, qk=qk, vd=vd),
        grid=(n, nc),
        in_specs=[row_spec(d), row_spec(d), row_spec(2 * d), row_spec(2 * d), s_spec,
                  _const_spec(dmask.shape), _const_spec(qdec.shape), _const_spec(kdec.shape), _const_spec(cdec.shape)],
        out_specs=[row_spec(2 * d), s_spec],
        out_shape=[jax.ShapeDtypeStruct((rows, 2 * d), BF16), jax.ShapeDtypeStruct(s0.shape, F32)],
        compiler_params=_params(2),
        name="ret_core",
    )(q, k, v, gate, s0, dmask, qdec, kdec, cdec)


def _ret_consts(l_real, l_pad, qk, vd):
    log_g = jnp.log1p(-jnp.exp2(-5.0 - jnp.arange(RET_HEADS, dtype=F32)))
    idx = jnp.arange(l_real, dtype=F32)
    diff = idx[:, None] - idx[None, :]
    dmask = jnp.exp(jnp.maximum(diff, 0.0)[None] * log_g[:, None, None]) * (diff >= 0)[None]
    q_dec = jnp.exp((idx + 1.0)[:, None] * log_g[None, :])
    k_dec = jnp.exp((l_real - 1.0 - idx)[:, None] * log_g[None, :])
    chunk_dec = jnp.exp(l_real * log_g)
    pad = l_pad - l_real
    dmask = jnp.pad(dmask, ((0, 0), (0, pad), (0, pad)))
    q_dec = jnp.pad(q_dec.T, ((0, 0), (0, pad)))
    k_dec = jnp.pad(k_dec.T, ((0, 0), (0, pad)))
    qdec = jnp.broadcast_to(q_dec[:, :, None], (RET_HEADS, l_pad, vd))
    kdec = jnp.broadcast_to(k_dec[:, :, None], (RET_HEADS, l_pad, qk))
    cdec = jnp.broadcast_to(chunk_dec[:, None, None], (RET_HEADS, 1, vd))
    return dmask, qdec, kdec, cdec


def _rope_tables(pos0, t, qk):
    half = qk // 2
    inv = ROPE_BASE ** (-jnp.arange(half, dtype=F32) / half)
    pos = pos0 + jnp.arange(t)
    ang = pos.astype(F32)[:, None] * inv[None, :]
    return jnp.cos(ang), jnp.sin(ang)


def _pool_body(x_ref, g_ref, buf_ref, w_ref, sc_ref, o_ref, tail_out, tail, *, nb, tc, pos0):
    i = pl.program_id(0)

    @pl.when(i == 0)
    def _():
        tail[...] = buf_ref[...]

    x = x_ref[...]
    h = _rms(x, g_ref[...])
    ext = jnp.concatenate([tail[...], h], axis=0)
    rows = tc * nb
    grp = x.shape[1] // len(POOL_WINDOWS)
    t_idx = i * tc + lax.shift_right_logical(lax.broadcasted_iota(jnp.int32, (rows, grp), 0), nb.bit_length() - 1)
    for gi, w in enumerate(POOL_WINDOWS):
        cols = slice(gi * grp, (gi + 1) * grp)
        s, first, span = ext[:, cols], 0, 1
        while span < w:
            s = s[span * nb:] + s[:-span * nb]
            first += span
            span *= 2
        s = s[(POOL_TAIL - first) * nb:]
        cnt = jnp.minimum(w, pos0 + t_idx + 1).astype(F32)
        pooled = s / cnt - h[:, cols]
        z = jnp.dot(pooled.astype(BF16), w_ref[gi], preferred_element_type=F32)
        o_ref[:, cols] = x[:, cols] + z * sc_ref[:, cols]
    new_tail = ext[rows:]
    tail[...] = new_tail

    @pl.when(i == pl.num_programs(0) - 1)
    def _():
        tail_out[...] = new_tail


def _pool(xt, g, layer, buf_t, w, scale, j, nb, tc, pos0):
    rows, d = xt.shape
    assert nb & (nb - 1) == 0
    r = tc * nb
    tail_rows = POOL_TAIL * nb
    return pl.pallas_call(
        functools.partial(_pool_body, nb=nb, tc=tc, pos0=pos0),
        grid=(rows // r,),
        in_specs=[pl.BlockSpec((r, d), lambda i: (i, 0)), _layer_spec(g.shape, layer), _const_spec((tail_rows, d)),
                  _layer_spec(w.shape, j), _layer_spec(scale.shape, j)],
        out_specs=[pl.BlockSpec((r, d), lambda i: (i, 0)), pl.BlockSpec((tail_rows, d), lambda i: (0, 0))],
        out_shape=[jax.ShapeDtypeStruct((rows, d), F32), jax.ShapeDtypeStruct((tail_rows, d), F32)],
        scratch_shapes=[pltpu.VMEM((tail_rows, d), F32)],
        compiler_params=_params(1),
        name="pool_mixer",
    )(xt, g, buf_t, w, scale)


def _relayout(x, n, t, src, dst):
    if src == dst:
        return x
    lead = (n, t) if src == 'bm' else (t, n)
    return x.reshape(lead + (-1,)).transpose(1, 0, 2).reshape(n * t, -1)


def _mixer_layout(layer):
    return 'bm' if layer % 3 == 1 else 'tm'


def _trunk(x, pos0, st_re, st_im, st_ret, st_pool, wts):
    n, t, d = x.shape
    rows = n * t
    depth = wts['norm_mix'].shape[0]
    qk, vd = d // RET_HEADS, 2 * d // RET_HEADS
    tc = min(max(ROW_TILE // n, 1), t)
    in_kernel_relayout = _can_relayout_in_kernel(n, t)
    new_re, new_im, new_ret, new_pool = [], [], [], []

    xb, layout = x.reshape(rows, d), 'bm'
    for layer in range(depth):
        kind, j = layer % 3, layer // 3
        want = _mixer_layout(layer)
        if layout != want and not (kind == 0 and in_kernel_relayout):
            xb, layout = _relayout(xb, n, t, layout, want), want
        if kind == 1:
            l_real = math.gcd(t, RET_CHUNK)
            nc = t // l_real
            lb = max(l_real, BF16_ROWS)
            cos, sin = _rope_tables(pos0, t, qk)
            if t < ROW_TILE:
                cos, sin = jnp.tile(cos, (ROW_TILE // t, 1)), jnp.tile(sin, (ROW_TILE // t, 1))
            q, k, v, gate = _ret_in(xb, wts['norm_mix'], layer, wts['ret_w_in'], j, cos, sin)
            if lb != l_real:
                pad = lambda z: jnp.pad(z.reshape(n * nc, l_real, -1), ((0, 0), (0, lb - l_real), (0, 0))
                                        ).reshape(n * nc * lb, -1)
                q, k, v, gate = pad(q), pad(k), pad(v), pad(gate)
            consts = _ret_consts(l_real, RET_CHUNK, qk, vd)
            a, s_new = _ret_core(q, k, v, gate, st_ret[j], consts, n, nc, lb)
            if lb != l_real:
                a = a.reshape(n * nc, lb, -1)[:, :l_real].reshape(rows, -1)
            proj, w_proj = 'plain', wts['ret_w_out']
            new_ret.append(s_new)
        elif kind == 0:
            bw, a_re, a_im, cw_re, cw_im = wts['s5_packed'][j]
            a, s_re, s_im = _s5_core(xb, layout, wts['norm_mix'], layer, bw, a_re, a_im, cw_re, cw_im,
                                     wts['s5_d'][j], st_re[j].reshape(n, -1), st_im[j].reshape(n, -1), n, t, tc)
            proj, w_proj = 'glu', wts['s5_w_glu']
            new_re.append(s_re.reshape(st_re[j].shape))
            new_im.append(s_im.reshape(st_im[j].shape))
        else:
            buf = st_pool[j]
            buf_t = jnp.pad(buf.transpose(1, 0, 2), ((1, 0), (0, 0), (0, 0))).reshape(POOL_TAIL * n, d)
            xb, tail = _pool(xb, wts['norm_mix'], layer, buf_t, wts['pool_w'], wts['pool_scale'], j, n, tc, pos0)
            a, proj, w_proj = None, None, None
            new_pool.append(tail.reshape(POOL_TAIL, n, d)[1:].transpose(1, 0, 2))
        final = layer == depth - 1
        wanted = 'bm' if final else _mixer_layout(layer + 1)
        dst = wanted if in_kernel_relayout else layout
        xb = _mlp(a, want, w_proj, j, proj, xb, layout, wts['norm_ffn'], wts['mlp_w_up'], wts['mlp_w_down'], layer,
                  wts['norm_final'], final, n, t, dst)
        layout = dst
    y = _relayout(xb, n, t, layout, 'bm').reshape(n, t, d)
    return y, jnp.stack(new_re), jnp.stack(new_im), jnp.stack(new_ret), jnp.stack(new_pool)


def kernel(x_prompt, x_sample, state_s5_re, state_s5_im, state_ret, state_pool, norm_mix, norm_ffn, norm_final,
           s5_a_re, s5_a_im, s5_log_dt, s5_b_re, s5_b_im, s5_c_re, s5_c_im, s5_d, s5_w_glu, ret_w_in, ret_w_out,
           pool_w, pool_scale, mlp_w_up, mlp_w_down):
    nb = x_prompt.shape[0]
    depth, d = norm_mix.shape
    wts = dict(
        norm_mix=norm_mix.reshape(depth, 1, d), norm_ffn=norm_ffn.reshape(depth, 1, d), norm_final=norm_final,
        s5_d=s5_d,
        s5_packed=[_s5_weights(s5_a_re[j], s5_a_im[j], s5_log_dt[j], s5_b_re[j], s5_b_im[j], s5_c_re[j], s5_c_im[j])
                   for j in range(s5_a_re.shape[0])],
        s5_w_glu=s5_w_glu.astype(BF16), ret_w_in=ret_w_in.astype(BF16), ret_w_out=ret_w_out.astype(BF16),
        pool_w=pool_w.astype(BF16), pool_scale=pool_scale.reshape(pool_scale.shape[0], 1, d),
        mlp_w_up=mlp_w_up.astype(BF16), mlp_w_down=mlp_w_down.astype(BF16))
    z_re = jnp.zeros((state_s5_re.shape[0], nb) + state_s5_re.shape[2:], state_s5_re.dtype)
    z_im = jnp.zeros((state_s5_im.shape[0], nb) + state_s5_im.shape[2:], state_s5_im.dtype)
    z_ret = jnp.zeros((state_ret.shape[0], nb) + state_ret.shape[2:], state_ret.dtype)
    z_pool = jnp.zeros((state_pool.shape[0], nb) + state_pool.shape[2:], state_pool.dtype)
    y_p, p_re, p_im, p_ret, p_pool = _trunk(x_prompt, 0, z_re, z_im, z_ret, z_pool, wts)
    y_s, s_re, s_im, s_ret, s_pool = _trunk(x_sample, PAST_LEN, state_s5_re, state_s5_im, state_ret, state_pool, wts)
    return (y_p, y_s, p_re, p_im, p_ret, p_pool, s_re, s_im, s_ret, s_pool)
```

```python
import functools
import math

import jax
import jax.numpy as jnp
from jax import lax
from jax.experimental import pallas as pl
from jax.experimental.pallas import tpu as pltpu

F32 = jnp.float32
BF16 = jnp.bfloat16

EPS = 1e-6
RET_HEADS = 4
RET_CHUNK = 128
ROPE_BASE = 10000.0
POOL_WINDOWS = (2, 4, 8, 16)
POOL_TAIL = 16

SUBLANES = 8
BF16_ROWS = 16
MXU_DIM = 256
ROW_TILE = 512
SCAN_LANES = 512
RET_SEQS_PER_STEP = 2
VMEM_LIMIT = 56 * 1024 * 1024
PAST_LEN = 16384


def _params(n_axes):
    return pltpu.CompilerParams(dimension_semantics=("arbitrary",) * n_axes, vmem_limit_bytes=VMEM_LIMIT)


def _const_spec(shape):
    zeros = (0,) * len(shape)
    return pl.BlockSpec(shape, lambda *_: zeros, pipeline_mode=pl.Buffered(1))


def _layer_spec(stacked_shape, layer):
    idx = (layer,) + (0,) * (len(stacked_shape) - 1)
    return pl.BlockSpec((None,) + tuple(stacked_shape[1:]), lambda *_: idx, pipeline_mode=pl.Buffered(1))


def _can_relayout_in_kernel(n, t):
    steps = ROW_TILE // n
    return ROW_TILE % n == 0 and steps % BF16_ROWS == 0 and t % steps == 0


def _tile_spec(layout, mixed, n, width):
    if mixed and layout == 'bm':
        return pl.BlockSpec((n, ROW_TILE // n, width), lambda i: (0, i, 0))
    return pl.BlockSpec((ROW_TILE, width), lambda i: (i, 0))


def _tile_view(x, layout, mixed, n, t):
    return x.reshape(n, t, x.shape[-1]) if mixed and layout == 'bm' else x


def _permute_rows(y, n, to):
    rows, d = y.shape
    lead = (rows // n, n) if to == 'bm' else (n, rows // n)
    return jnp.swapaxes(y.reshape(lead + (d,)), 0, 1).reshape(rows, d)


def _rms(x, g):
    ms = jnp.mean(x * x, axis=-1, keepdims=True)
    return x * lax.rsqrt(ms + EPS) * g


def _mlp_body(*refs, proj, final, ff_chunk, n, mix_to, out_to):
    if proj is None:
        x_ref, g_ref, wu_ref, wd_ref, gf_ref, o_ref = refs
    else:
        a_ref, wp_ref, x_ref, g_ref, wu_ref, wd_ref, gf_ref, o_ref = refs
    d = x_ref.shape[-1]
    x = x_ref[...].reshape(ROW_TILE, d)
    if proj is not None:
        mix = jnp.dot(a_ref[...].reshape(ROW_TILE, a_ref.shape[-1]), wp_ref[...], preferred_element_type=F32)
        if proj == 'glu':
            mix = mix[:, :d] * jax.nn.sigmoid(mix[:, d:])
        if mix_to is not None:
            mix = _permute_rows(mix, n, mix_to)
        x = x + mix
    h = _rms(x, g_ref[...]).astype(BF16)
    acc = jnp.zeros_like(x)
    d_ff = wu_ref.shape[1]
    for c in range(d_ff // ff_chunk):
        cols = slice(c * ff_chunk, (c + 1) * ff_chunk)
        a = jnp.dot(h, wu_ref[:, cols], preferred_element_type=F32)
        a = jnp.square(jnp.maximum(a, 0.0)).astype(BF16)
        acc = acc + jnp.dot(a, wd_ref[cols, :], preferred_element_type=F32)
    y = x + acc
    if final:
        y = _rms(y, gf_ref[...])
    if out_to is not None:
        y = _permute_rows(y, n, out_to)
    o_ref[...] = y.reshape(o_ref.shape)


def _mlp(a, a_layout, w_proj, j, proj, x, x_layout, g, w_up, w_down, layer, g_final, final, n, t, dst):
    d = x.shape[1]
    layouts = {x_layout, dst} | ({a_layout} if proj is not None else set())
    mixed = len(layouts) > 1
    assert not mixed or _can_relayout_in_kernel(n, t)
    operands, specs = [], []
    if proj is not None:
        operands += [_tile_view(a, a_layout, mixed, n, t), w_proj]
        specs += [_tile_spec(a_layout, mixed, n, a.shape[1]), _layer_spec(w_proj.shape, j)]
    operands += [_tile_view(x, x_layout, mixed, n, t), g, w_up, w_down, g_final.reshape(1, d)]
    specs += [_tile_spec(x_layout, mixed, n, d), _layer_spec(g.shape, layer), _layer_spec(w_up.shape, layer),
              _layer_spec(w_down.shape, layer), _const_spec((1, d))]
    out_shape = (n, t, d) if mixed and dst == 'bm' else (n * t, d)
    y = pl.pallas_call(
        functools.partial(_mlp_body, proj=proj, final=final, ff_chunk=1024, n=n,
                          mix_to=x_layout if proj is not None and a_layout != x_layout else None,
                          out_to=dst if dst != x_layout else None),
        grid=(n * t // ROW_TILE,),
        in_specs=specs,
        out_specs=_tile_spec(dst, mixed, n, d),
        out_shape=jax.ShapeDtypeStruct(out_shape, F32),
        compiler_params=_params(1),
        name=("mlp" if proj is None else proj + "_mlp") + ("_final" if final else ""),
    )(*operands)
    return y.reshape(n * t, d)


def _s5_body(x_ref, g_ref, bw_ref, are_ref, aim_ref, cre_ref, cim_ref, dsk_ref, s0re_ref, s0im_ref,
             gl_ref, sre_out, sim_out, bu_re, bu_im, st_re, st_im, *, nb, tc):
    i = pl.program_id(0)

    @pl.when(i == 0)
    def _():
        st_re[...] = s0re_ref[...]
        st_im[...] = s0im_ref[...]

    d = x_ref.shape[-1]
    x = x_ref[...].reshape(tc * nb, d)
    if len(x_ref.shape) == 3:
        x = _permute_rows(x, nb, 'tm')
    h = _rms(x, g_ref[...])
    hb = h.astype(BF16)
    n_state_slab = bw_ref.shape[2] // 2
    ys = []
    for j in range(bw_ref.shape[0]):
        r = jnp.dot(hb[:, j * MXU_DIM:(j + 1) * MXU_DIM], bw_ref[j], preferred_element_type=F32)
        cols = slice(j * n_state_slab, (j + 1) * n_state_slab)
        bu_re[:, cols] = r[:, :n_state_slab]
        bu_im[:, cols] = r[:, n_state_slab:]
        for c in range(n_state_slab // SCAN_LANES):
            lanes = slice(j * n_state_slab + c * SCAN_LANES, j * n_state_slab + (c + 1) * SCAN_LANES)
            ar = jnp.broadcast_to(are_ref[:, lanes], (SUBLANES, SCAN_LANES))
            ai = jnp.broadcast_to(aim_ref[:, lanes], (SUBLANES, SCAN_LANES))
            for rg in range(nb // SUBLANES):
                seqs = slice(rg * SUBLANES, (rg + 1) * SUBLANES)
                sr, si = st_re[seqs, lanes], st_im[seqs, lanes]
                for t in range(tc):
                    rows = slice(t * nb + rg * SUBLANES, t * nb + (rg + 1) * SUBLANES)
                    sr, si = (ar * sr - ai * si + bu_re[rows, lanes], ar * si + ai * sr + bu_im[rows, lanes])
                    bu_re[rows, lanes] = sr
                    bu_im[rows, lanes] = si
                st_re[seqs, lanes] = sr
                st_im[seqs, lanes] = si
        y = jnp.dot(bu_re[:, cols].astype(BF16), cre_ref[j], preferred_element_type=F32)
        ys.append(y + jnp.dot(bu_im[:, cols].astype(BF16), cim_ref[j], preferred_element_type=F32))
    y = jnp.concatenate(ys, axis=1) + dsk_ref[...] * h
    gl_ref[...] = jax.nn.gelu(y, approximate=True).astype(BF16)

    @pl.when(i == pl.num_programs(0) - 1)
    def _():
        sre_out[...] = st_re[...]
        sim_out[...] = st_im[...]


def _s5_core(x, x_layout, g, layer, packed, d_skip, j, s0_re, s0_im, nb, t, tc):
    rows, d = x.shape
    bw, a_re, a_im, c_re, c_im = packed
    n_lanes = a_re.shape[-1]
    r = tc * nb
    from_bm = x_layout == 'bm'
    assert not from_bm or (r == ROW_TILE and _can_relayout_in_kernel(nb, t))
    return pl.pallas_call(
        functools.partial(_s5_body, nb=nb, tc=tc),
        grid=(rows // r,),
        in_specs=[_tile_spec(x_layout, from_bm, nb, d) if from_bm else pl.BlockSpec((r, d), lambda i: (i, 0)),
                  _layer_spec(g.shape, layer), _layer_spec(bw.shape, j), _layer_spec(a_re.shape, j),
                  _layer_spec(a_im.shape, j), _layer_spec(c_re.shape, j), _layer_spec(c_im.shape, j),
                  _layer_spec(d_skip.shape, j),
                  _const_spec((nb, n_lanes)), _const_spec((nb, n_lanes))],
        out_specs=[pl.BlockSpec((r, d), lambda i: (i, 0)),
                   pl.BlockSpec((nb, n_lanes), lambda i: (0, 0)), pl.BlockSpec((nb, n_lanes), lambda i: (0, 0))],
        out_shape=[jax.ShapeDtypeStruct((rows, d), BF16),
                   jax.ShapeDtypeStruct((nb, n_lanes), F32), jax.ShapeDtypeStruct((nb, n_lanes), F32)],
        scratch_shapes=[pltpu.VMEM((r, n_lanes), F32), pltpu.VMEM((r, n_lanes), F32),
                        pltpu.VMEM((nb, n_lanes), F32), pltpu.VMEM((nb, n_lanes), F32)],
        compiler_params=_params(1),
        name="s5_core",
    )(_tile_view(x, x_layout, from_bm, nb, t), g, bw, a_re, a_im, c_re, c_im, d_skip, s0_re, s0_im)


def _s5_weights(a_re, a_im, log_dt, b_re, b_im, c_re, c_im):
    groups, p, hch = b_re.shape
    dt = jnp.exp(log_dt.astype(F32))[:, None]
    lr, li = a_re.astype(F32), a_im.astype(F32)
    mag = jnp.exp(lr * dt)
    ab_re, ab_im = mag * jnp.cos(li * dt), mag * jnp.sin(li * dt)
    den = lr * lr + li * li
    nr, ni = ab_re - 1.0, ab_im
    f_re = (nr * lr + ni * li) / den
    f_im = (ni * lr - nr * li) / den
    br, bi = b_re.astype(F32), b_im.astype(F32)
    bb_re = f_re[..., None] * br - f_im[..., None] * bi
    bb_im = f_re[..., None] * bi + f_im[..., None] * br
    gs = MXU_DIM // hch
    n_slab = groups // gs

    def block_diag(per_group):
        r, c = per_group.shape[1:]
        tiled = jnp.tile(per_group.reshape(n_slab, gs * r, c), (1, 1, gs))
        row_g = lax.broadcasted_iota(jnp.int32, (gs * r, gs * c), 0) // r
        col_g = lax.broadcasted_iota(jnp.int32, (gs * r, gs * c), 1) // c
        return jnp.where(row_g == col_g, tiled, 0.0).astype(BF16)

    to_hp = lambda bb: bb.transpose(0, 2, 1)
    bw = jnp.concatenate([block_diag(to_hp(bb_re)), block_diag(to_hp(bb_im))], axis=2)
    cw_re = block_diag(c_re.astype(F32).transpose(0, 2, 1))
    cw_im = block_diag(-c_im.astype(F32).transpose(0, 2, 1))
    return bw, ab_re.reshape(1, groups * p), ab_im.reshape(1, groups * p), cw_re, cw_im


def _ret_in_body(x_ref, g_ref, w_ref, cos_ref, sin_ref, q_ref, k_ref, v_ref, gate_ref, *, d, qk):
    h = _rms(x_ref[...], g_ref[...]).astype(BF16)
    cos, sin = cos_ref[...], sin_ref[...]
    half = qk // 2

    def rot(z):
        outs = []
        for hd in range(d // qk):
            z1 = z[:, hd * qk:hd * qk + half]
            z2 = z[:, hd * qk + half:(hd + 1) * qk]
            outs += [z1 * cos - z2 * sin, z1 * sin + z2 * cos]
        return jnp.concatenate(outs, axis=1)

    q = jnp.dot(h, w_ref[:, 0:d], preferred_element_type=F32)
    q_ref[...] = (rot(q) * (qk ** -0.5)).astype(BF16)
    k = jnp.dot(h, w_ref[:, d:2 * d], preferred_element_type=F32)
    k_ref[...] = rot(k).astype(BF16)
    v_ref[...] = jnp.dot(h, w_ref[:, 2 * d:4 * d], preferred_element_type=F32).astype(BF16)
    gate_ref[...] = jnp.dot(h, w_ref[:, 4 * d:6 * d], preferred_element_type=F32).astype(BF16)


def _ret_in(x, g, layer, w_in, j, cos_tab, sin_tab):
    rows, d = x.shape
    qk = d // RET_HEADS
    tab_blocks = cos_tab.shape[0] // ROW_TILE
    row_spec = lambda width: pl.BlockSpec((ROW_TILE, width), lambda i: (i, 0))
    tab_spec = pl.BlockSpec((ROW_TILE, qk // 2), lambda i: (i % tab_blocks, 0))
    return pl.pallas_call(
        functools.partial(_ret_in_body, d=d, qk=qk),
        grid=(rows // ROW_TILE,),
        in_specs=[row_spec(d), _layer_spec(g.shape, layer), _layer_spec(w_in.shape, j), tab_spec, tab_spec],
        out_specs=[row_spec(d), row_spec(d), row_spec(2 * d), row_spec(2 * d)],
        out_shape=[jax.ShapeDtypeStruct((rows, d), BF16), jax.ShapeDtypeStruct((rows, d), BF16),
                   jax.ShapeDtypeStruct((rows, 2 * d), BF16), jax.ShapeDtypeStruct((rows, 2 * d), BF16)],
        compiler_params=_params(1),
        name="ret_in",
    )(x, g, w_in, cos_tab, sin_tab)


def _ret_core_body(q_ref, k_ref, v_ref, gate_ref, s0_ref, dmask_ref, qdec_ref, kdec_ref, cdec_ref,
                   og_ref, s_ref, *, lb, lp, qk, vd, seqs, chunks):
    c = pl.program_id(1)

    @pl.when(c == 0)
    def _():
        s_ref[...] = s0_ref[...]

    def padded(z):
        if lb == lp:
            return z
        return jnp.concatenate([z, jnp.zeros((lp - lb, z.shape[1]), z.dtype)], axis=0)

    for blk in range(seqs * chunks):
        b = blk // chunks
        rows = slice(blk * lb, (blk + 1) * lb)
        q, k, v = padded(q_ref[rows, :]), padded(k_ref[rows, :]), padded(v_ref[rows, :])
        gate = padded(gate_ref[rows, :])
        for hd in range(RET_HEADS):
            qh = q[:, hd * qk:(hd + 1) * qk]
            kh = k[:, hd * qk:(hd + 1) * qk]
            kf = kh.astype(F32)
            vh = v[:, hd * vd:(hd + 1) * vd]
            s_old = s_ref[b, hd]
            scores = lax.dot_general(qh, kh, (((1,), (1,)), ((), ())), preferred_element_type=F32) * dmask_ref[hd]
            o = jnp.dot(scores.astype(BF16), vh, preferred_element_type=F32)
            o = o + qdec_ref[hd] * jnp.dot(qh, s_old.astype(BF16), preferred_element_type=F32)
            kd = (kf * kdec_ref[hd]).astype(BF16)
            s_ref[b, hd] = cdec_ref[hd] * s_old + lax.dot_general(kd, vh, (((0,), (0,)), ((), ())),
                                                                  preferred_element_type=F32)
            mu = jnp.mean(o, axis=-1, keepdims=True)
            oc = o - mu
            o = oc * lax.rsqrt(jnp.mean(oc * oc, axis=-1, keepdims=True) + EPS)
            og = o * jax.nn.silu(gate[:, hd * vd:(hd + 1) * vd].astype(F32))
            og_ref[rows, hd * vd:(hd + 1) * vd] = og[:lb].astype(BF16)


def _ret_core(q, k, v, gate, s0, consts, n, nc, lb):
    rows, d = q.shape
    qk, vd = d // RET_HEADS, 2 * d // RET_HEADS
    dmask, qdec, kdec, cdec = consts
    lp = dmask.shape[1]
    chunks = math.gcd(nc, max(ROW_TILE // lb, 1))
    seqs = math.gcd(n, RET_SEQS_PER_STEP) if chunks == nc else 1
    nc_blocks = nc // chunks
    row_spec = lambda width: pl.BlockSpec((seqs * chunks * lb, width), lambda b, c: (b * nc_blocks + c, 0))
    s_spec = pl.BlockSpec((seqs, RET_HEADS, qk, vd), lambda b, c: (b, 0, 0, 0))
    return pl.pallas_call(
        functools.partial(_ret_core_body, lb=lb, lp=lp, qk=qk, vd=vd, seqs=seqs, chunks=chunks),
        grid=(n // seqs, nc_blocks),
        in_specs=[row_spec(d), row_spec(d), row_spec(2 * d), row_spec(2 * d), s_spec,
                  _const_spec(dmask.shape), _const_spec(qdec.shape), _const_spec(kdec.shape), _const_spec(cdec.shape)],
        out_specs=[row_spec(2 * d), s_spec],
        out_shape=[jax.ShapeDtypeStruct((rows, 2 * d), BF16), jax.ShapeDtypeStruct(s0.shape, F32)],
        compiler_params=_params(2),
        name="ret_core",
    )(q, k, v, gate, s0, dmask, qdec, kdec, cdec)


def _ret_consts(l_real, l_pad, qk, vd):
    log_g = jnp.log1p(-jnp.exp2(-5.0 - jnp.arange(RET_HEADS, dtype=F32)))
    idx = jnp.arange(l_real, dtype=F32)
    diff = idx[:, None] - idx[None, :]
    dmask = jnp.exp(jnp.maximum(diff, 0.0)[None] * log_g[:, None, None]) * (diff >= 0)[None]
    q_dec = jnp.exp((idx + 1.0)[:, None] * log_g[None, :])
    k_dec = jnp.exp((l_real - 1.0 - idx)[:, None] * log_g[None, :])
    chunk_dec = jnp.exp(l_real * log_g)
    pad = l_pad - l_real
    dmask = jnp.pad(dmask, ((0, 0), (0, pad), (0, pad)))
    q_dec = jnp.pad(q_dec.T, ((0, 0), (0, pad)))
    k_dec = jnp.pad(k_dec.T, ((0, 0), (0, pad)))
    qdec = jnp.broadcast_to(q_dec[:, :, None], (RET_HEADS, l_pad, vd))
    kdec = jnp.broadcast_to(k_dec[:, :, None], (RET_HEADS, l_pad, qk))
    cdec = jnp.broadcast_to(chunk_dec[:, None, None], (RET_HEADS, 1, vd))
    return dmask, qdec, kdec, cdec


def _rope_tables(pos0, t, qk):
    half = qk // 2
    inv = ROPE_BASE ** (-jnp.arange(half, dtype=F32) / half)
    pos = pos0 + jnp.arange(t)
    ang = pos.astype(F32)[:, None] * inv[None, :]
    return jnp.cos(ang), jnp.sin(ang)


def _pool_body(x_ref, g_ref, buf_ref, w_ref, sc_ref, o_ref, tail_out, tail, *, nb, tc, pos0):
    i = pl.program_id(0)

    @pl.when(i == 0)
    def _():
        tail[...] = buf_ref[...]

    x = x_ref[...]
    h = _rms(x, g_ref[...])
    ext = jnp.concatenate([tail[...], h], axis=0)
    rows = tc * nb
    grp = x.shape[1] // len(POOL_WINDOWS)
    t_idx = i * tc + lax.shift_right_logical(lax.broadcasted_iota(jnp.int32, (rows, grp), 0), nb.bit_length() - 1)
    for gi, w in enumerate(POOL_WINDOWS):
        cols = slice(gi * grp, (gi + 1) * grp)
        s, first, span = ext[:, cols], 0, 1
        while span < w:
            s = s[span * nb:] + s[:-span * nb]
            first += span
            span *= 2
        s = s[(POOL_TAIL - first) * nb:]
        cnt = jnp.minimum(w, pos0 + t_idx + 1).astype(F32)
        pooled = s / cnt - h[:, cols]
        z = jnp.dot(pooled.astype(BF16), w_ref[gi], preferred_element_type=F32)
        o_ref[:, cols] = x[:, cols] + z * sc_ref[:, cols]
    new_tail = ext[rows:]
    tail[...] = new_tail

    @pl.when(i == pl.num_programs(0) - 1)
    def _():
        tail_out[...] = new_tail


def _pool(xt, g, layer, buf_t, w, scale, j, nb, tc, pos0):
    rows, d = xt.shape
    assert nb & (nb - 1) == 0
    r = tc * nb
    tail_rows = POOL_TAIL * nb
    return pl.pallas_call(
        functools.partial(_pool_body, nb=nb, tc=tc, pos0=pos0),
        grid=(rows // r,),
        in_specs=[pl.BlockSpec((r, d), lambda i: (i, 0)), _layer_spec(g.shape, layer), _const_spec((tail_rows, d)),
                  _layer_spec(w.shape, j), _layer_spec(scale.shape, j)],
        out_specs=[pl.BlockSpec((r, d), lambda i: (i, 0)), pl.BlockSpec((tail_rows, d), lambda i: (0, 0))],
        out_shape=[jax.ShapeDtypeStruct((rows, d), F32), jax.ShapeDtypeStruct((tail_rows, d), F32)],
        scratch_shapes=[pltpu.VMEM((tail_rows, d), F32)],
        compiler_params=_params(1),
        name="pool_mixer",
    )(xt, g, buf_t, w, scale)


def _relayout(x, n, t, src, dst):
    if src == dst:
        return x
    lead = (n, t) if src == 'bm' else (t, n)
    return x.reshape(lead + (-1,)).transpose(1, 0, 2).reshape(n * t, -1)


def _mixer_layout(layer):
    return 'bm' if layer % 3 == 1 else 'tm'


def _trunk(x, pos0, st_re, st_im, st_ret, st_pool, wts):
    n, t, d = x.shape
    rows = n * t
    depth = wts['norm_mix'].shape[0]
    qk, vd = d // RET_HEADS, 2 * d // RET_HEADS
    tc = min(max(ROW_TILE // n, 1), t)
    in_kernel_relayout = _can_relayout_in_kernel(n, t)
    new_re, new_im, new_ret, new_pool = [], [], [], []

    xb, layout = x.reshape(rows, d), 'bm'
    for layer in range(depth):
        kind, j = layer % 3, layer // 3
        want = _mixer_layout(layer)
        if layout != want and not (kind == 0 and in_kernel_relayout):
            xb, layout = _relayout(xb, n, t, layout, want), want
        if kind == 1:
            l_real = math.gcd(t, RET_CHUNK)
            nc = t // l_real
            lb = max(l_real, BF16_ROWS)
            cos, sin = _rope_tables(pos0, t, qk)
            if t < ROW_TILE:
                cos, sin = jnp.tile(cos, (ROW_TILE // t, 1)), jnp.tile(sin, (ROW_TILE // t, 1))
            q, k, v, gate = _ret_in(xb, wts['norm_mix'], layer, wts['ret_w_in'], j, cos, sin)
            if lb != l_real:
                pad = lambda z: jnp.pad(z.reshape(n * nc, l_real, -1), ((0, 0), (0, lb - l_real), (0, 0))
                                        ).reshape(n * nc * lb, -1)
                q, k, v, gate = pad(q), pad(k), pad(v), pad(gate)
            consts = _ret_consts(l_real, RET_CHUNK, qk, vd)
            a, s_new = _ret_core(q, k, v, gate, st_ret[j], consts, n, nc, lb)
            if lb != l_real:
                a = a.reshape(n * nc, lb, -1)[:, :l_real].reshape(rows, -1)
            proj, w_proj = 'plain', wts['ret_w_out']
            new_ret.append(s_new)
        elif kind == 0:
            a, s_re, s_im = _s5_core(xb, layout, wts['norm_mix'], layer, wts['s5_packed'], wts['s5_d'], j,
                                     st_re[j].reshape(n, -1), st_im[j].reshape(n, -1), n, t, tc)
            proj, w_proj = 'glu', wts['s5_w_glu']
            new_re.append(s_re.reshape(st_re[j].shape))
            new_im.append(s_im.reshape(st_im[j].shape))
        else:
            buf = st_pool[j]
            buf_t = jnp.pad(buf.transpose(1, 0, 2), ((1, 0), (0, 0), (0, 0))).reshape(POOL_TAIL * n, d)
            xb, tail = _pool(xb, wts['norm_mix'], layer, buf_t, wts['pool_w'], wts['pool_scale'], j, n, tc, pos0)
            a, proj, w_proj = None, None, None
            new_pool.append(tail.reshape(POOL_TAIL, n, d)[1:].transpose(1, 0, 2))
        final = layer == depth - 1
        wanted = 'bm' if final else _mixer_layout(layer + 1)
        dst = wanted if in_kernel_relayout else layout
        xb = _mlp(a, want, w_proj, j, proj, xb, layout, wts['norm_ffn'], wts['mlp_w_up'], wts['mlp_w_down'], layer,
                  wts['norm_final'], final, n, t, dst)
        layout = dst
    y = _relayout(xb, n, t, layout, 'bm').reshape(n, t, d)
    return y, jnp.stack(new_re), jnp.stack(new_im), jnp.stack(new_ret), jnp.stack(new_pool)


def kernel(x_prompt, x_sample, state_s5_re, state_s5_im, state_ret, state_pool, norm_mix, norm_ffn, norm_final,
           s5_a_re, s5_a_im, s5_log_dt, s5_b_re, s5_b_im, s5_c_re, s5_c_im, s5_d, s5_w_glu, ret_w_in, ret_w_out,
           pool_w, pool_scale, mlp_w_up, mlp_w_down):
    nb = x_prompt.shape[0]
    depth, d = norm_mix.shape
    wts = dict(
        norm_mix=norm_mix.reshape(depth, 1, d), norm_ffn=norm_ffn.reshape(depth, 1, d), norm_final=norm_final,
        s5_d=s5_d.reshape(s5_d.shape[0], 1, d),
        s5_packed=jax.vmap(_s5_weights)(s5_a_re, s5_a_im, s5_log_dt, s5_b_re, s5_b_im, s5_c_re, s5_c_im),
        s5_w_glu=s5_w_glu.astype(BF16), ret_w_in=ret_w_in.astype(BF16), ret_w_out=ret_w_out.astype(BF16),
        pool_w=pool_w.astype(BF16), pool_scale=pool_scale.reshape(pool_scale.shape[0], 1, d),
        mlp_w_up=mlp_w_up.astype(BF16), mlp_w_down=mlp_w_down.astype(BF16))
    z_re = jnp.zeros((state_s5_re.shape[0], nb) + state_s5_re.shape[2:], state_s5_re.dtype)
    z_im = jnp.zeros((state_s5_im.shape[0], nb) + state_s5_im.shape[2:], state_s5_im.dtype)
    z_ret = jnp.zeros((state_ret.shape[0], nb) + state_ret.shape[2:], state_ret.dtype)
    z_pool = jnp.zeros((state_pool.shape[0], nb) + state_pool.shape[2:], state_pool.dtype)
    y_p, p_re, p_im, p_ret, p_pool = _trunk(x_prompt, 0, z_re, z_im, z_ret, z_pool, wts)
    y_s, s_re, s_im, s_ret, s_pool = _trunk(x_sample, PAST_LEN, state_s5_re, state_s5_im, state_ret, state_pool, wts)
    return (y_p, y_s, p_re, p_im, p_ret, p_pool, s_re, s_im, s_ret, s_pool)
```

```python
import functools
import math

import jax
import jax.numpy as jnp
from jax import lax
from jax.experimental import pallas as pl
from jax.experimental.pallas import tpu as pltpu

F32 = jnp.float32
BF16 = jnp.bfloat16

EPS = 1e-6
RET_HEADS = 4
RET_CHUNK = 256
RET_MIN_CHUNK_ROWS = 128
ROPE_BASE = 10000.0
POOL_WINDOWS = (2, 4, 8, 16)
POOL_TAIL = 16

SUBLANES = 8
BF16_ROWS = 16
MXU_DIM = 256
ROW_TILE = 512
SCAN_LANES = 512
RET_SEQS_PER_STEP = 2
VMEM_LIMIT = 56 * 1024 * 1024
PAST_LEN = 16384


def _params(n_axes):
    return pltpu.CompilerParams(dimension_semantics=("arbitrary",) * n_axes, vmem_limit_bytes=VMEM_LIMIT)


def _const_spec(shape):
    zeros = (0,) * len(shape)
    return pl.BlockSpec(shape, lambda *_: zeros, pipeline_mode=pl.Buffered(1))


def _layer_spec(stacked_shape, layer):
    idx = (layer,) + (0,) * (len(stacked_shape) - 1)
    return pl.BlockSpec((None,) + tuple(stacked_shape[1:]), lambda *_: idx, pipeline_mode=pl.Buffered(1))


def _can_relayout_in_kernel(n, t):
    steps = ROW_TILE // n
    return ROW_TILE % n == 0 and steps % BF16_ROWS == 0 and t % steps == 0


def _tile_spec(layout, mixed, n, width):
    if mixed and layout == 'bm':
        return pl.BlockSpec((n, ROW_TILE // n, width), lambda i: (0, i, 0))
    return pl.BlockSpec((ROW_TILE, width), lambda i: (i, 0))


def _tile_view(x, layout, mixed, n, t):
    return x.reshape(n, t, x.shape[-1]) if mixed and layout == 'bm' else x


def _permute_rows(y, n, to):
    rows, d = y.shape
    lead = (rows // n, n) if to == 'bm' else (n, rows // n)
    return jnp.swapaxes(y.reshape(lead + (d,)), 0, 1).reshape(rows, d)


def _rms(x, g):
    ms = jnp.mean(x * x, axis=-1, keepdims=True)
    return x * lax.rsqrt(ms + EPS) * g


def _gated_heads_proj(o, gate, wp_ref):
    vd = o.shape[1] // RET_HEADS
    mix = None
    for hd in range(RET_HEADS):
        cols = slice(hd * vd, (hd + 1) * vd)
        oc = o[:, cols] - jnp.mean(o[:, cols], axis=-1, keepdims=True)
        on = oc * lax.rsqrt(jnp.mean(oc * oc, axis=-1, keepdims=True) + EPS)
        og = (on * jax.nn.silu(gate[:, cols].astype(F32))).astype(BF16)
        part = jnp.dot(og, wp_ref[cols, :], preferred_element_type=F32)
        mix = part if mix is None else mix + part
    return mix


def _mlp_body(*refs, proj, final, ff_chunk, n, mix_to, out_to):
    n_mix_refs = {None: 0, 'glu': 2, 'gated': 3}[proj]
    x_ref, g_ref, wu_ref, wd_ref, gf_ref, o_ref = refs[n_mix_refs:]
    d = x_ref.shape[-1]
    x = x_ref[...].reshape(ROW_TILE, d)
    if proj is not None:
        a_ref, wp_ref = refs[0], refs[n_mix_refs - 1]
        a = a_ref[...].reshape(ROW_TILE, a_ref.shape[-1])
        if proj == 'glu':
            mix = jnp.dot(a, wp_ref[...], preferred_element_type=F32)
            mix = mix[:, :d] * jax.nn.sigmoid(mix[:, d:])
        else:
            mix = _gated_heads_proj(a, refs[1][...].reshape(a.shape), wp_ref)
        if mix_to is not None:
            mix = _permute_rows(mix, n, mix_to)
        x = x + mix
    h = _rms(x, g_ref[...]).astype(BF16)
    acc = jnp.zeros_like(x)
    d_ff = wu_ref.shape[1]
    for c in range(d_ff // ff_chunk):
        cols = slice(c * ff_chunk, (c + 1) * ff_chunk)
        a = jnp.dot(h, wu_ref[:, cols], preferred_element_type=F32)
        a = jnp.square(jnp.maximum(a, 0.0)).astype(BF16)
        acc = acc + jnp.dot(a, wd_ref[cols, :], preferred_element_type=F32)
    y = x + acc
    if final:
        y = _rms(y, gf_ref[...])
    if out_to is not None:
        y = _permute_rows(y, n, out_to)
    o_ref[...] = y.reshape(o_ref.shape)


def _mlp(a, a_layout, w_proj, j, proj, x, x_layout, g, w_up, w_down, layer, g_final, final, n, t, dst):
    d = x.shape[1]
    layouts = {x_layout, dst} | ({a_layout} if proj is not None else set())
    mixed = len(layouts) > 1
    assert not mixed or _can_relayout_in_kernel(n, t)
    operands, specs = [], []
    if proj is not None:
        operands += [_tile_view(z, a_layout, mixed, n, t) for z in a] + [w_proj]
        specs += [_tile_spec(a_layout, mixed, n, z.shape[1]) for z in a] + [_layer_spec(w_proj.shape, j)]
    operands += [_tile_view(x, x_layout, mixed, n, t), g, w_up, w_down, g_final.reshape(1, d)]
    specs += [_tile_spec(x_layout, mixed, n, d), _layer_spec(g.shape, layer), _layer_spec(w_up.shape, layer),
              _layer_spec(w_down.shape, layer), _const_spec((1, d))]
    out_shape = (n, t, d) if mixed and dst == 'bm' else (n * t, d)
    y = pl.pallas_call(
        functools.partial(_mlp_body, proj=proj, final=final, ff_chunk=1024, n=n,
                          mix_to=x_layout if proj is not None and a_layout != x_layout else None,
                          out_to=dst if dst != x_layout else None),
        grid=(n * t // ROW_TILE,),
        in_specs=specs,
        out_specs=_tile_spec(dst, mixed, n, d),
        out_shape=jax.ShapeDtypeStruct(out_shape, F32),
        compiler_params=_params(1),
        name=("mlp" if proj is None else proj + "_mlp") + ("_final" if final else ""),
    )(*operands)
    return y.reshape(n * t, d)


def _s5_body(x_ref, g_ref, bw_ref, are_ref, aim_ref, cre_ref, cim_ref, dsk_ref, s0re_ref, s0im_ref,
             gl_ref, sre_out, sim_out, bu_re, bu_im, st_re, st_im, *, nb, tc):
    i = pl.program_id(0)

    @pl.when(i == 0)
    def _():
        st_re[...] = s0re_ref[...]
        st_im[...] = s0im_ref[...]

    d = x_ref.shape[-1]
    x = x_ref[...].reshape(tc * nb, d)
    if len(x_ref.shape) == 3:
        x = _permute_rows(x, nb, 'tm')
    h = _rms(x, g_ref[...])
    hb = h.astype(BF16)
    n_state_slab = bw_ref.shape[2] // 2
    ys = []
    for j in range(bw_ref.shape[0]):
        r = jnp.dot(hb[:, j * MXU_DIM:(j + 1) * MXU_DIM], bw_ref[j], preferred_element_type=F32)
        cols = slice(j * n_state_slab, (j + 1) * n_state_slab)
        bu_re[:, cols] = r[:, :n_state_slab]
        bu_im[:, cols] = r[:, n_state_slab:]
        for c in range(n_state_slab // SCAN_LANES):
            lanes = slice(j * n_state_slab + c * SCAN_LANES, j * n_state_slab + (c + 1) * SCAN_LANES)
            ar = jnp.broadcast_to(are_ref[:, lanes], (SUBLANES, SCAN_LANES))
            ai = jnp.broadcast_to(aim_ref[:, lanes], (SUBLANES, SCAN_LANES))
            for rg in range(nb // SUBLANES):
                seqs = slice(rg * SUBLANES, (rg + 1) * SUBLANES)
                sr, si = st_re[seqs, lanes], st_im[seqs, lanes]
                for t in range(tc):
                    rows = slice(t * nb + rg * SUBLANES, t * nb + (rg + 1) * SUBLANES)
                    sr, si = (ar * sr - ai * si + bu_re[rows, lanes], ar * si + ai * sr + bu_im[rows, lanes])
                    bu_re[rows, lanes] = sr
                    bu_im[rows, lanes] = si
                st_re[seqs, lanes] = sr
                st_im[seqs, lanes] = si
        y = jnp.dot(bu_re[:, cols].astype(BF16), cre_ref[j], preferred_element_type=F32)
        ys.append(y + jnp.dot(bu_im[:, cols].astype(BF16), cim_ref[j], preferred_element_type=F32))
    y = jnp.concatenate(ys, axis=1) + dsk_ref[...] * h
    gl_ref[...] = jax.nn.gelu(y, approximate=True).astype(BF16)

    @pl.when(i == pl.num_programs(0) - 1)
    def _():
        sre_out[...] = st_re[...]
        sim_out[...] = st_im[...]


def _s5_core(x, x_layout, g, layer, packed, d_skip, j, s0_re, s0_im, nb, t, tc):
    rows, d = x.shape
    bw, a_re, a_im, c_re, c_im = packed
    n_lanes = a_re.shape[-1]
    r = tc * nb
    from_bm = x_layout == 'bm'
    assert not from_bm or (r == ROW_TILE and _can_relayout_in_kernel(nb, t))
    return pl.pallas_call(
        functools.partial(_s5_body, nb=nb, tc=tc),
        grid=(rows // r,),
        in_specs=[_tile_spec(x_layout, from_bm, nb, d) if from_bm else pl.BlockSpec((r, d), lambda i: (i, 0)),
                  _layer_spec(g.shape, layer), _layer_spec(bw.shape, j), _layer_spec(a_re.shape, j),
                  _layer_spec(a_im.shape, j), _layer_spec(c_re.shape, j), _layer_spec(c_im.shape, j),
                  _layer_spec(d_skip.shape, j),
                  _const_spec((nb, n_lanes)), _const_spec((nb, n_lanes))],
        out_specs=[pl.BlockSpec((r, d), lambda i: (i, 0)),
                   pl.BlockSpec((nb, n_lanes), lambda i: (0, 0)), pl.BlockSpec((nb, n_lanes), lambda i: (0, 0))],
        out_shape=[jax.ShapeDtypeStruct((rows, d), BF16),
                   jax.ShapeDtypeStruct((nb, n_lanes), F32), jax.ShapeDtypeStruct((nb, n_lanes), F32)],
        scratch_shapes=[pltpu.VMEM((r, n_lanes), F32), pltpu.VMEM((r, n_lanes), F32),
                        pltpu.VMEM((nb, n_lanes), F32), pltpu.VMEM((nb, n_lanes), F32)],
        compiler_params=_params(1),
        name="s5_core",
    )(_tile_view(x, x_layout, from_bm, nb, t), g, bw, a_re, a_im, c_re, c_im, d_skip, s0_re, s0_im)


def _s5_weights(a_re, a_im, log_dt, b_re, b_im, c_re, c_im):
    groups, p, hch = b_re.shape
    dt = jnp.exp(log_dt.astype(F32))[:, None]
    lr, li = a_re.astype(F32), a_im.astype(F32)
    mag = jnp.exp(lr * dt)
    ab_re, ab_im = mag * jnp.cos(li * dt), mag * jnp.sin(li * dt)
    den = lr * lr + li * li
    nr, ni = ab_re - 1.0, ab_im
    f_re = (nr * lr + ni * li) / den
    f_im = (ni * lr - nr * li) / den
    br, bi = b_re.astype(F32), b_im.astype(F32)
    bb_re = f_re[..., None] * br - f_im[..., None] * bi
    bb_im = f_re[..., None] * bi + f_im[..., None] * br
    gs = MXU_DIM // hch
    n_slab = groups // gs

    def block_diag(per_group):
        r, c = per_group.shape[1:]
        tiled = jnp.tile(per_group.reshape(n_slab, gs * r, c), (1, 1, gs))
        row_g = lax.broadcasted_iota(jnp.int32, (gs * r, gs * c), 0) // r
        col_g = lax.broadcasted_iota(jnp.int32, (gs * r, gs * c), 1) // c
        return jnp.where(row_g == col_g, tiled, 0.0).astype(BF16)

    to_hp = lambda bb: bb.transpose(0, 2, 1)
    bw = jnp.concatenate([block_diag(to_hp(bb_re)), block_diag(to_hp(bb_im))], axis=2)
    cw_re = block_diag(c_re.astype(F32).transpose(0, 2, 1))
    cw_im = block_diag(-c_im.astype(F32).transpose(0, 2, 1))
    return bw, ab_re.reshape(1, groups * p), ab_im.reshape(1, groups * p), cw_re, cw_im


def _ret_in_body(x_ref, g_ref, w_ref, cos_ref, sin_ref, q_ref, k_ref, v_ref, gate_ref, *, d, qk):
    h = _rms(x_ref[...], g_ref[...]).astype(BF16)
    cos, sin = cos_ref[...], sin_ref[...]
    half = qk // 2

    def rot(z):
        outs = []
        for hd in range(d // qk):
            z1 = z[:, hd * qk:hd * qk + half]
            z2 = z[:, hd * qk + half:(hd + 1) * qk]
            outs += [z1 * cos - z2 * sin, z1 * sin + z2 * cos]
        return jnp.concatenate(outs, axis=1)

    q = jnp.dot(h, w_ref[:, 0:d], preferred_element_type=F32)
    q_ref[...] = (rot(q) * (qk ** -0.5)).astype(BF16)
    k = jnp.dot(h, w_ref[:, d:2 * d], preferred_element_type=F32)
    k_ref[...] = rot(k).astype(BF16)
    v_ref[...] = jnp.dot(h, w_ref[:, 2 * d:4 * d], preferred_element_type=F32).astype(BF16)
    gate_ref[...] = jnp.dot(h, w_ref[:, 4 * d:6 * d], preferred_element_type=F32).astype(BF16)


def _ret_in(x, g, layer, w_in, j, cos_tab, sin_tab):
    rows, d = x.shape
    qk = d // RET_HEADS
    tab_blocks = cos_tab.shape[0] // ROW_TILE
    row_spec = lambda width: pl.BlockSpec((ROW_TILE, width), lambda i: (i, 0))
    tab_spec = pl.BlockSpec((ROW_TILE, qk // 2), lambda i: (i % tab_blocks, 0))
    return pl.pallas_call(
        functools.partial(_ret_in_body, d=d, qk=qk),
        grid=(rows // ROW_TILE,),
        in_specs=[row_spec(d), _layer_spec(g.shape, layer), _layer_spec(w_in.shape, j), tab_spec, tab_spec],
        out_specs=[row_spec(d), row_spec(d), row_spec(2 * d), row_spec(2 * d)],
        out_shape=[jax.ShapeDtypeStruct((rows, d), BF16), jax.ShapeDtypeStruct((rows, d), BF16),
                   jax.ShapeDtypeStruct((rows, 2 * d), BF16), jax.ShapeDtypeStruct((rows, 2 * d), BF16)],
        compiler_params=_params(1),
        name="ret_in",
    )(x, g, w_in, cos_tab, sin_tab)


def _ret_core_body(q_ref, k_ref, v_ref, s0_ref, dmask_ref, qdec_ref, kdec_ref, cdec_ref,
                   o_ref, s_ref, *, lb, lp, qk, vd, seqs, chunks):
    c = pl.program_id(1)

    @pl.when(c == 0)
    def _():
        s_ref[...] = s0_ref[...]

    def padded(z):
        if lb == lp:
            return z
        return jnp.concatenate([z, jnp.zeros((lp - lb, z.shape[1]), z.dtype)], axis=0)

    for blk in range(seqs * chunks):
        b = blk // chunks
        rows = slice(blk * lb, (blk + 1) * lb)
        q, k, v = padded(q_ref[rows, :]), padded(k_ref[rows, :]), padded(v_ref[rows, :])
        for hd in range(RET_HEADS):
            qh = q[:, hd * qk:(hd + 1) * qk]
            kh = k[:, hd * qk:(hd + 1) * qk]
            kf = kh.astype(F32)
            vh = v[:, hd * vd:(hd + 1) * vd]
            s_old = s_ref[b, hd]
            scores = lax.dot_general(qh, kh, (((1,), (1,)), ((), ())), preferred_element_type=F32) * dmask_ref[hd]
            o = jnp.dot(scores.astype(BF16), vh, preferred_element_type=F32)
            o = o + qdec_ref[hd] * jnp.dot(qh, s_old.astype(BF16), preferred_element_type=F32)
            kd = (kf * kdec_ref[hd]).astype(BF16)
            s_ref[b, hd] = cdec_ref[hd] * s_old + lax.dot_general(kd, vh, (((0,), (0,)), ((), ())),
                                                                  preferred_element_type=F32)
            o_ref[rows, hd * vd:(hd + 1) * vd] = o[:lb]


def _ret_core(q, k, v, s0, consts, n, nc, lb):
    rows, d = q.shape
    qk, vd = d // RET_HEADS, 2 * d // RET_HEADS
    dmask, qdec, kdec, cdec = consts
    lp = dmask.shape[1]
    chunks = math.gcd(nc, max(ROW_TILE // lb, 1))
    seqs = math.gcd(n, RET_SEQS_PER_STEP) if chunks == nc else 1
    nc_blocks = nc // chunks
    row_spec = lambda width: pl.BlockSpec((seqs * chunks * lb, width), lambda b, c: (b * nc_blocks + c, 0))
    s_spec = pl.BlockSpec((seqs, RET_HEADS, qk, vd), lambda b, c: (b, 0, 0, 0))
    return pl.pallas_call(
        functools.partial(_ret_core_body, lb=lb, lp=lp, qk=qk, vd=vd, seqs=seqs, chunks=chunks),
        grid=(n // seqs, nc_blocks),
        in_specs=[row_spec(d), row_spec(d), row_spec(2 * d), s_spec,
                  _const_spec(dmask.shape), _const_spec(qdec.shape), _const_spec(kdec.shape), _const_spec(cdec.shape)],
        out_specs=[row_spec(2 * d), s_spec],
        out_shape=[jax.ShapeDtypeStruct((rows, 2 * d), F32), jax.ShapeDtypeStruct(s0.shape, F32)],
        compiler_params=_params(2),
        name="ret_core",
    )(q, k, v, s0, dmask, qdec, kdec, cdec)


def _ret_consts(l_real, l_pad, qk, vd):
    log_g = jnp.log1p(-jnp.exp2(-5.0 - jnp.arange(RET_HEADS, dtype=F32)))
    idx = jnp.arange(l_real, dtype=F32)
    diff = idx[:, None] - idx[None, :]
    dmask = jnp.exp(jnp.maximum(diff, 0.0)[None] * log_g[:, None, None]) * (diff >= 0)[None]
    q_dec = jnp.exp((idx + 1.0)[:, None] * log_g[None, :])
    k_dec = jnp.exp((l_real - 1.0 - idx)[:, None] * log_g[None, :])
    chunk_dec = jnp.exp(l_real * log_g)
    pad = l_pad - l_real
    dmask = jnp.pad(dmask, ((0, 0), (0, pad), (0, pad)))
    q_dec = jnp.pad(q_dec.T, ((0, 0), (0, pad)))
    k_dec = jnp.pad(k_dec.T, ((0, 0), (0, pad)))
    qdec = jnp.broadcast_to(q_dec[:, :, None], (RET_HEADS, l_pad, vd))
    kdec = jnp.broadcast_to(k_dec[:, :, None], (RET_HEADS, l_pad, qk))
    cdec = jnp.broadcast_to(chunk_dec[:, None, None], (RET_HEADS, 1, vd))
    return dmask, qdec, kdec, cdec


def _rope_tables(pos0, t, qk):
    half = qk // 2
    inv = ROPE_BASE ** (-jnp.arange(half, dtype=F32) / half)
    pos = pos0 + jnp.arange(t)
    ang = pos.astype(F32)[:, None] * inv[None, :]
    return jnp.cos(ang), jnp.sin(ang)


def _pool_body(x_ref, g_ref, buf_ref, w_ref, sc_ref, o_ref, tail_out, tail, *, nb, tc, pos0):
    i = pl.program_id(0)

    @pl.when(i == 0)
    def _():
        tail[...] = buf_ref[...]

    x = x_ref[...]
    h = _rms(x, g_ref[...])
    ext = jnp.concatenate([tail[...], h], axis=0)
    rows = tc * nb
    grp = x.shape[1] // len(POOL_WINDOWS)
    t_idx = i * tc + lax.shift_right_logical(lax.broadcasted_iota(jnp.int32, (rows, grp), 0), nb.bit_length() - 1)
    for gi, w in enumerate(POOL_WINDOWS):
        cols = slice(gi * grp, (gi + 1) * grp)
        s, first, span = ext[:, cols], 0, 1
        while span < w:
            s = s[span * nb:] + s[:-span * nb]
            first += span
            span *= 2
        s = s[(POOL_TAIL - first) * nb:]
        cnt = jnp.minimum(w, pos0 + t_idx + 1).astype(F32)
        pooled = s / cnt - h[:, cols]
        z = jnp.dot(pooled.astype(BF16), w_ref[gi], preferred_element_type=F32)
        o_ref[:, cols] = x[:, cols] + z * sc_ref[:, cols]
    new_tail = ext[rows:]
    tail[...] = new_tail

    @pl.when(i == pl.num_programs(0) - 1)
    def _():
        tail_out[...] = new_tail


def _pool(xt, g, layer, buf_t, w, scale, j, nb, tc, pos0):
    rows, d = xt.shape
    assert nb & (nb - 1) == 0
    r = tc * nb
    tail_rows = POOL_TAIL * nb
    return pl.pallas_call(
        functools.partial(_pool_body, nb=nb, tc=tc, pos0=pos0),
        grid=(rows // r,),
        in_specs=[pl.BlockSpec((r, d), lambda i: (i, 0)), _layer_spec(g.shape, layer), _const_spec((tail_rows, d)),
                  _layer_spec(w.shape, j), _layer_spec(scale.shape, j)],
        out_specs=[pl.BlockSpec((r, d), lambda i: (i, 0)), pl.BlockSpec((tail_rows, d), lambda i: (0, 0))],
        out_shape=[jax.ShapeDtypeStruct((rows, d), F32), jax.ShapeDtypeStruct((tail_rows, d), F32)],
        scratch_shapes=[pltpu.VMEM((tail_rows, d), F32)],
        compiler_params=_params(1),
        name="pool_mixer",
    )(xt, g, buf_t, w, scale)


def _relayout(x, n, t, src, dst):
    if src == dst:
        return x
    lead = (n, t) if src == 'bm' else (t, n)
    return x.reshape(lead + (-1,)).transpose(1, 0, 2).reshape(n * t, -1)


def _mixer_layout(layer):
    return 'bm' if layer % 3 == 1 else 'tm'


def _trunk(x, pos0, st_re, st_im, st_ret, st_pool, wts):
    n, t, d = x.shape
    rows = n * t
    depth = wts['norm_mix'].shape[0]
    qk, vd = d // RET_HEADS, 2 * d // RET_HEADS
    tc = min(max(ROW_TILE // n, 1), t)
    in_kernel_relayout = _can_relayout_in_kernel(n, t)
    new_re, new_im, new_ret, new_pool = [], [], [], []

    xb, layout = x.reshape(rows, d), 'bm'
    for layer in range(depth):
        kind, j = layer % 3, layer // 3
        want = _mixer_layout(layer)
        if layout != want and not (kind == 0 and in_kernel_relayout):
            xb, layout = _relayout(xb, n, t, layout, want), want
        if kind == 1:
            l_real = math.gcd(t, RET_CHUNK)
            nc = t // l_real
            lb = max(l_real, BF16_ROWS)
            cos, sin = _rope_tables(pos0, t, qk)
            if t < ROW_TILE:
                cos, sin = jnp.tile(cos, (ROW_TILE // t, 1)), jnp.tile(sin, (ROW_TILE // t, 1))
            q, k, v, gate = _ret_in(xb, wts['norm_mix'], layer, wts['ret_w_in'], j, cos, sin)
            if lb != l_real:
                pad = lambda z: jnp.pad(z.reshape(n * nc, l_real, -1), ((0, 0), (0, lb - l_real), (0, 0))
                                        ).reshape(n * nc * lb, -1)
                q, k, v = pad(q), pad(k), pad(v)
            consts = _ret_consts(l_real, max(l_real, RET_MIN_CHUNK_ROWS), qk, vd)
            o, s_new = _ret_core(q, k, v, st_ret[j], consts, n, nc, lb)
            if lb != l_real:
                o = o.reshape(n * nc, lb, -1)[:, :l_real].reshape(rows, -1)
            a, proj, w_proj = (o, gate), 'gated', wts['ret_w_out']
            new_ret.append(s_new)
        elif kind == 0:
            gl, s_re, s_im = _s5_core(xb, layout, wts['norm_mix'], layer, wts['s5_packed'], wts['s5_d'], j,
                                      st_re[j].reshape(n, -1), st_im[j].reshape(n, -1), n, t, tc)
            a, proj, w_proj = (gl,), 'glu', wts['s5_w_glu']
            new_re.append(s_re.reshape(st_re[j].shape))
            new_im.append(s_im.reshape(st_im[j].shape))
        else:
            buf = st_pool[j]
            buf_t = jnp.pad(buf.transpose(1, 0, 2), ((1, 0), (0, 0), (0, 0))).reshape(POOL_TAIL * n, d)
            xb, tail = _pool(xb, wts['norm_mix'], layer, buf_t, wts['pool_w'], wts['pool_scale'], j, n, tc, pos0)
            a, proj, w_proj = (), None, None
            new_pool.append(tail.reshape(POOL_TAIL, n, d)[1:].transpose(1, 0, 2))
        final = layer == depth - 1
        wanted = 'bm' if final else _mixer_layout(layer + 1)
        dst = wanted if in_kernel_relayout else layout
        xb = _mlp(a, want, w_proj, j, proj, xb, layout, wts['norm_ffn'], wts['mlp_w_up'], wts['mlp_w_down'], layer,
                  wts['norm_final'], final, n, t, dst)
        layout = dst
    y = _relayout(xb, n, t, layout, 'bm').reshape(n, t, d)
    return y, jnp.stack(new_re), jnp.stack(new_im), jnp.stack(new_ret), jnp.stack(new_pool)


def kernel(x_prompt, x_sample, state_s5_re, state_s5_im, state_ret, state_pool, norm_mix, norm_ffn, norm_final,
           s5_a_re, s5_a_im, s5_log_dt, s5_b_re, s5_b_im, s5_c_re, s5_c_im, s5_d, s5_w_glu, ret_w_in, ret_w_out,
           pool_w, pool_scale, mlp_w_up, mlp_w_down):
    nb = x_prompt.shape[0]
    depth, d = norm_mix.shape
    wts = dict(
        norm_mix=norm_mix.reshape(depth, 1, d), norm_ffn=norm_ffn.reshape(depth, 1, d), norm_final=norm_final,
        s5_d=s5_d.reshape(s5_d.shape[0], 1, d),
        s5_packed=jax.vmap(_s5_weights)(s5_a_re, s5_a_im, s5_log_dt, s5_b_re, s5_b_im, s5_c_re, s5_c_im),
        s5_w_glu=s5_w_glu.astype(BF16), ret_w_in=ret_w_in.astype(BF16), ret_w_out=ret_w_out.astype(BF16),
        pool_w=pool_w.astype(BF16), pool_scale=pool_scale.reshape(pool_scale.shape[0], 1, d),
        mlp_w_up=mlp_w_up.astype(BF16), mlp_w_down=mlp_w_down.astype(BF16))
    z_re = jnp.zeros((state_s5_re.shape[0], nb) + state_s5_re.shape[2:], state_s5_re.dtype)
    z_im = jnp.zeros((state_s5_im.shape[0], nb) + state_s5_im.shape[2:], state_s5_im.dtype)
    z_ret = jnp.zeros((state_ret.shape[0], nb) + state_ret.shape[2:], state_ret.dtype)
    z_pool = jnp.zeros((state_pool.shape[0], nb) + state_pool.shape[2:], state_pool.dtype)
    y_p, p_re, p_im, p_ret, p_pool = _trunk(x_prompt, 0, z_re, z_im, z_ret, z_pool, wts)
    y_s, s_re, s_im, s_ret, s_pool = _trunk(x_sample, PAST_LEN, state_s5_re, state_s5_im, state_ret, state_pool, wts)
    return (y_p, y_s, p_re, p_im, p_ret, p_pool, s_re, s_im, s_ret, s_pool)
```

```python
import functools
import math

import jax
import jax.numpy as jnp
from jax import lax
from jax.experimental import pallas as pl
from jax.experimental.pallas import tpu as pltpu

F32 = jnp.float32
BF16 = jnp.bfloat16

EPS = 1e-6
RET_HEADS = 4
RET_CHUNK = 256
RET_MIN_CHUNK_ROWS = 128
ROPE_BASE = 10000.0
POOL_WINDOWS = (2, 4, 8, 16)
POOL_TAIL = 16

SUBLANES = 8
BF16_ROWS = 16
MXU_DIM = 256
ROW_TILE = 512
SCAN_LANES = 512
RET_SEQS_PER_STEP = 4
POOL_FUSE_MAX_TAIL_BYTES = 1 << 20
VMEM_LIMIT = 56 * 1024 * 1024
PAST_LEN = 16384


def _params(n_axes):
    return pltpu.CompilerParams(dimension_semantics=("arbitrary",) * n_axes, vmem_limit_bytes=VMEM_LIMIT)


def _const_spec(shape):
    zeros = (0,) * len(shape)
    return pl.BlockSpec(shape, lambda *_: zeros, pipeline_mode=pl.Buffered(1))


def _layer_spec(stacked_shape, layer):
    idx = (layer,) + (0,) * (len(stacked_shape) - 1)
    return pl.BlockSpec((None,) + tuple(stacked_shape[1:]), lambda *_: idx, pipeline_mode=pl.Buffered(1))


def _can_relayout_in_kernel(n, t):
    steps = ROW_TILE // n
    return ROW_TILE % n == 0 and steps % BF16_ROWS == 0 and t % steps == 0


def _tile_spec(layout, mixed, n, width):
    if mixed and layout == 'bm':
        return pl.BlockSpec((n, ROW_TILE // n, width), lambda i: (0, i, 0))
    return pl.BlockSpec((ROW_TILE, width), lambda i: (i, 0))


def _tile_view(x, layout, mixed, n, t):
    return x.reshape(n, t, x.shape[-1]) if mixed and layout == 'bm' else x


def _permute_rows(y, n, to):
    rows, d = y.shape
    lead = (rows // n, n) if to == 'bm' else (n, rows // n)
    return jnp.swapaxes(y.reshape(lead + (d,)), 0, 1).reshape(rows, d)


def _rms(x, g):
    ms = jnp.mean(x * x, axis=-1, keepdims=True)
    return x * lax.rsqrt(ms + EPS) * g


def _gated_heads_proj(o, gate, wp_ref):
    vd = o.shape[1] // RET_HEADS
    mix = None
    for hd in range(RET_HEADS):
        cols = slice(hd * vd, (hd + 1) * vd)
        oc = o[:, cols] - jnp.mean(o[:, cols], axis=-1, keepdims=True)
        on = oc * lax.rsqrt(jnp.mean(oc * oc, axis=-1, keepdims=True) + EPS)
        og = (on * jax.nn.silu(gate[:, cols].astype(F32))).astype(BF16)
        part = jnp.dot(og, wp_ref[cols, :], preferred_element_type=F32)
        mix = part if mix is None else mix + part
    return mix


def _pool_mix(x, g_ref, buf_ref, w_ref, sc_ref, tail_out, tail, *, n, pos0):
    i = pl.program_id(0)

    @pl.when(i == 0)
    def _():
        tail[...] = buf_ref[...]

    rows, d = x.shape
    tc = rows // n
    h = _rms(x, g_ref[...])
    ext = jnp.concatenate([tail[...], h], axis=0)
    grp = d // len(POOL_WINDOWS)
    t_idx = i * tc + lax.shift_right_logical(lax.broadcasted_iota(jnp.int32, (rows, grp), 0), n.bit_length() - 1)
    zs = []
    for gi, w in enumerate(POOL_WINDOWS):
        cols = slice(gi * grp, (gi + 1) * grp)
        s, first, span = ext[:, cols], 0, 1
        while span < w:
            s = s[span * n:] + s[:-span * n]
            first += span
            span *= 2
        s = s[(POOL_TAIL - first) * n:]
        cnt = jnp.minimum(w, pos0 + t_idx + 1).astype(F32)
        pooled = s / cnt - h[:, cols]
        zs.append(jnp.dot(pooled.astype(BF16), w_ref[gi], preferred_element_type=F32))
    new_tail = ext[rows:]
    tail[...] = new_tail

    @pl.when(i == pl.num_programs(0) - 1)
    def _():
        tail_out[...] = new_tail

    return jnp.concatenate(zs, axis=1) * sc_ref[...]


MIXER_REFS = {None: 0, 'glu': 2, 'gated': 3, 'pool': 4}


def _mlp_body(*refs, proj, final, ff_chunk, n, mix_to, out_to, pos0):
    mix_refs = refs[:MIXER_REFS[proj]]
    x_ref, g_ref, wu_ref, wd_ref, gf_ref, o_ref = refs[len(mix_refs):len(mix_refs) + 6]
    d = x_ref.shape[-1]
    x = x_ref[...].reshape(ROW_TILE, d)
    if proj == 'pool':
        x = x + _pool_mix(x, *mix_refs, *refs[len(mix_refs) + 6:], n=n, pos0=pos0)
    elif proj is not None:
        a = mix_refs[0][...].reshape(ROW_TILE, mix_refs[0].shape[-1])
        if proj == 'glu':
            mix = jnp.dot(a, mix_refs[1][...], preferred_element_type=F32)
            mix = mix[:, :d] * jax.nn.sigmoid(mix[:, d:])
        else:
            mix = _gated_heads_proj(a, mix_refs[1][...].reshape(a.shape), mix_refs[2])
        if mix_to is not None:
            mix = _permute_rows(mix, n, mix_to)
        x = x + mix
    h = _rms(x, g_ref[...]).astype(BF16)
    acc = jnp.zeros_like(x)
    d_ff = wu_ref.shape[1]
    for c in range(d_ff // ff_chunk):
        cols = slice(c * ff_chunk, (c + 1) * ff_chunk)
        a = jnp.dot(h, wu_ref[:, cols], preferred_element_type=F32)
        a = jnp.square(jnp.maximum(a, 0.0)).astype(BF16)
        acc = acc + jnp.dot(a, wd_ref[cols, :], preferred_element_type=F32)
    y = x + acc
    if final:
        y = _rms(y, gf_ref[...])
    if out_to is not None:
        y = _permute_rows(y, n, out_to)
    o_ref[...] = y.reshape(o_ref.shape)


def _mlp(proj, mixer, a_layout, x, x_layout, g, w_up, w_down, layer, g_final, final, n, t, dst, pos0):
    d = x.shape[1]
    j = mixer[-1] if mixer else None
    mixed = len({x_layout, dst} | ({a_layout} if proj in ('glu', 'gated') else set())) > 1
    assert not mixed or _can_relayout_in_kernel(n, t)
    out_shape = [jax.ShapeDtypeStruct((n, t, d) if mixed and dst == 'bm' else (n * t, d), F32)]
    out_specs = [_tile_spec(dst, mixed, n, d)]
    scratch = []
    if proj == 'pool':
        assert x_layout == 'tm' and ROW_TILE % n == 0 and n & (n - 1) == 0
        g_mix, tail0, w_pool, scale = mixer[:-1]
        operands = [g_mix, tail0, w_pool, scale]
        specs = [_layer_spec(g_mix.shape, layer), _const_spec(tail0.shape), _layer_spec(w_pool.shape, j),
                 _layer_spec(scale.shape, j)]
        out_shape.append(jax.ShapeDtypeStruct(tail0.shape, F32))
        out_specs.append(pl.BlockSpec(tail0.shape, lambda i: (0, 0)))
        scratch.append(pltpu.VMEM(tail0.shape, F32))
    elif proj is None:
        operands, specs = [], []
    else:
        rows_in, w_proj = mixer[:-2], mixer[-2]
        operands = [_tile_view(z, a_layout, mixed, n, t) for z in rows_in] + [w_proj]
        specs = [_tile_spec(a_layout, mixed, n, z.shape[1]) for z in rows_in] + [_layer_spec(w_proj.shape, j)]
    operands += [_tile_view(x, x_layout, mixed, n, t), g, w_up, w_down, g_final.reshape(1, d)]
    specs += [_tile_spec(x_layout, mixed, n, d), _layer_spec(g.shape, layer), _layer_spec(w_up.shape, layer),
              _layer_spec(w_down.shape, layer), _const_spec((1, d))]
    outs = pl.pallas_call(
        functools.partial(_mlp_body, proj=proj, final=final, ff_chunk=1024, n=n, pos0=pos0,
                          mix_to=x_layout if proj in ('glu', 'gated') and a_layout != x_layout else None,
                          out_to=dst if dst != x_layout else None),
        grid=(n * t // ROW_TILE,),
        in_specs=specs,
        out_specs=out_specs,
        out_shape=out_shape,
        scratch_shapes=scratch,
        compiler_params=_params(1),
        name=(proj + "_mlp" if proj else "mlp") + ("_final" if final else ""),
    )(*operands)
    return (outs[0].reshape(n * t, d),) + tuple(outs[1:])


def _pool_body(g_ref, buf_ref, w_ref, sc_ref, x_ref, o_ref, tail_out, tail, *, n, pos0):
    x = x_ref[...]
    o_ref[...] = x + _pool_mix(x, g_ref, buf_ref, w_ref, sc_ref, tail_out, tail, n=n, pos0=pos0)


def _pool(x, g_mix, layer, tail0, w_pool, scale, j, n, pos0):
    rows, d = x.shape
    assert ROW_TILE % n == 0 and n & (n - 1) == 0
    return pl.pallas_call(
        functools.partial(_pool_body, n=n, pos0=pos0),
        grid=(rows // ROW_TILE,),
        in_specs=[_layer_spec(g_mix.shape, layer), _const_spec(tail0.shape), _layer_spec(w_pool.shape, j),
                  _layer_spec(scale.shape, j), pl.BlockSpec((ROW_TILE, d), lambda i: (i, 0))],
        out_specs=[pl.BlockSpec((ROW_TILE, d), lambda i: (i, 0)), pl.BlockSpec(tail0.shape, lambda i: (0, 0))],
        out_shape=[jax.ShapeDtypeStruct((rows, d), F32), jax.ShapeDtypeStruct(tail0.shape, F32)],
        scratch_shapes=[pltpu.VMEM(tail0.shape, F32)],
        compiler_params=_params(1),
        name="pool_mixer",
    )(g_mix, tail0, w_pool, scale, x)


def _s5_body(x_ref, g_ref, bw_ref, are_ref, aim_ref, cre_ref, cim_ref, dsk_ref, s0re_ref, s0im_ref,
             gl_ref, sre_out, sim_out, bu_re, bu_im, st_re, st_im, *, nb, tc):
    i = pl.program_id(0)

    @pl.when(i == 0)
    def _():
        st_re[...] = s0re_ref[...]
        st_im[...] = s0im_ref[...]

    d = x_ref.shape[-1]
    x = x_ref[...].reshape(tc * nb, d)
    if len(x_ref.shape) == 3:
        x = _permute_rows(x, nb, 'tm')
    h = _rms(x, g_ref[...])
    hb = h.astype(BF16)
    n_state_slab = bw_ref.shape[2] // 2
    ys = []
    for j in range(bw_ref.shape[0]):
        r = jnp.dot(hb[:, j * MXU_DIM:(j + 1) * MXU_DIM], bw_ref[j], preferred_element_type=F32)
        cols = slice(j * n_state_slab, (j + 1) * n_state_slab)
        bu_re[:, cols] = r[:, :n_state_slab]
        bu_im[:, cols] = r[:, n_state_slab:]
        for c in range(n_state_slab // SCAN_LANES):
            lanes = slice(j * n_state_slab + c * SCAN_LANES, j * n_state_slab + (c + 1) * SCAN_LANES)
            ar = jnp.broadcast_to(are_ref[:, lanes], (SUBLANES, SCAN_LANES))
            ai = jnp.broadcast_to(aim_ref[:, lanes], (SUBLANES, SCAN_LANES))
            for rg in range(nb // SUBLANES):
                seqs = slice(rg * SUBLANES, (rg + 1) * SUBLANES)
                sr, si = st_re[seqs, lanes], st_im[seqs, lanes]
                for t in range(tc):
                    rows = slice(t * nb + rg * SUBLANES, t * nb + (rg + 1) * SUBLANES)
                    sr, si = (ar * sr - ai * si + bu_re[rows, lanes], ar * si + ai * sr + bu_im[rows, lanes])
                    bu_re[rows, lanes] = sr
                    bu_im[rows, lanes] = si
                st_re[seqs, lanes] = sr
                st_im[seqs, lanes] = si
        y = jnp.dot(bu_re[:, cols].astype(BF16), cre_ref[j], preferred_element_type=F32)
        ys.append(y + jnp.dot(bu_im[:, cols].astype(BF16), cim_ref[j], preferred_element_type=F32))
    y = jnp.concatenate(ys, axis=1) + dsk_ref[...] * h
    gl_ref[...] = jax.nn.gelu(y, approximate=True).astype(BF16)

    @pl.when(i == pl.num_programs(0) - 1)
    def _():
        sre_out[...] = st_re[...]
        sim_out[...] = st_im[...]


def _s5_core(x, x_layout, g, layer, packed, d_skip, j, s0_re, s0_im, nb, t, tc):
    rows, d = x.shape
    bw, a_re, a_im, c_re, c_im = packed
    n_lanes = a_re.shape[-1]
    r = tc * nb
    from_bm = x_layout == 'bm'
    assert not from_bm or (r == ROW_TILE and _can_relayout_in_kernel(nb, t))
    return pl.pallas_call(
        functools.partial(_s5_body, nb=nb, tc=tc),
        grid=(rows // r,),
        in_specs=[_tile_spec(x_layout, from_bm, nb, d) if from_bm else pl.BlockSpec((r, d), lambda i: (i, 0)),
                  _layer_spec(g.shape, layer), _layer_spec(bw.shape, j), _layer_spec(a_re.shape, j),
                  _layer_spec(a_im.shape, j), _layer_spec(c_re.shape, j), _layer_spec(c_im.shape, j),
                  _layer_spec(d_skip.shape, j),
                  _const_spec((nb, n_lanes)), _const_spec((nb, n_lanes))],
        out_specs=[pl.BlockSpec((r, d), lambda i: (i, 0)),
                   pl.BlockSpec((nb, n_lanes), lambda i: (0, 0)), pl.BlockSpec((nb, n_lanes), lambda i: (0, 0))],
        out_shape=[jax.ShapeDtypeStruct((rows, d), BF16),
                   jax.ShapeDtypeStruct((nb, n_lanes), F32), jax.ShapeDtypeStruct((nb, n_lanes), F32)],
        scratch_shapes=[pltpu.VMEM((r, n_lanes), F32), pltpu.VMEM((r, n_lanes), F32),
                        pltpu.VMEM((nb, n_lanes), F32), pltpu.VMEM((nb, n_lanes), F32)],
        compiler_params=_params(1),
        name="s5_core",
    )(_tile_view(x, x_layout, from_bm, nb, t), g, bw, a_re, a_im, c_re, c_im, d_skip, s0_re, s0_im)


def _s5_weights(a_re, a_im, log_dt, b_re, b_im, c_re, c_im):
    groups, p, hch = b_re.shape
    dt = jnp.exp(log_dt.astype(F32))[:, None]
    lr, li = a_re.astype(F32), a_im.astype(F32)
    mag = jnp.exp(lr * dt)
    ab_re, ab_im = mag * jnp.cos(li * dt), mag * jnp.sin(li * dt)
    den = lr * lr + li * li
    nr, ni = ab_re - 1.0, ab_im
    f_re = (nr * lr + ni * li) / den
    f_im = (ni * lr - nr * li) / den
    br, bi = b_re.astype(F32), b_im.astype(F32)
    bb_re = f_re[..., None] * br - f_im[..., None] * bi
    bb_im = f_re[..., None] * bi + f_im[..., None] * br
    gs = MXU_DIM // hch
    n_slab = groups // gs

    def block_diag(per_group):
        r, c = per_group.shape[1:]
        tiled = jnp.tile(per_group.reshape(n_slab, gs * r, c), (1, 1, gs))
        row_g = lax.broadcasted_iota(jnp.int32, (gs * r, gs * c), 0) // r
        col_g = lax.broadcasted_iota(jnp.int32, (gs * r, gs * c), 1) // c
        return jnp.where(row_g == col_g, tiled, 0.0).astype(BF16)

    to_hp = lambda bb: bb.transpose(0, 2, 1)
    bw = jnp.concatenate([block_diag(to_hp(bb_re)), block_diag(to_hp(bb_im))], axis=2)
    cw_re = block_diag(c_re.astype(F32).transpose(0, 2, 1))
    cw_im = block_diag(-c_im.astype(F32).transpose(0, 2, 1))
    return bw, ab_re.reshape(1, groups * p), ab_im.reshape(1, groups * p), cw_re, cw_im


def _ret_in_body(x_ref, g_ref, w_ref, cos_ref, sin_ref, q_ref, k_ref, v_ref, gate_ref, *, d, qk):
    h = _rms(x_ref[...], g_ref[...]).astype(BF16)
    cos, sin = cos_ref[...], sin_ref[...]
    half = qk // 2

    def rot(z):
        outs = []
        for hd in range(d // qk):
            z1 = z[:, hd * qk:hd * qk + half]
            z2 = z[:, hd * qk + half:(hd + 1) * qk]
            outs += [z1 * cos - z2 * sin, z1 * sin + z2 * cos]
        return jnp.concatenate(outs, axis=1)

    q = jnp.dot(h, w_ref[:, 0:d], preferred_element_type=F32)
    q_ref[...] = (rot(q) * (qk ** -0.5)).astype(BF16)
    k = jnp.dot(h, w_ref[:, d:2 * d], preferred_element_type=F32)
    k_ref[...] = rot(k).astype(BF16)
    v_ref[...] = jnp.dot(h, w_ref[:, 2 * d:4 * d], preferred_element_type=F32).astype(BF16)
    gate_ref[...] = jnp.dot(h, w_ref[:, 4 * d:6 * d], preferred_element_type=F32).astype(BF16)


def _ret_in(x, g, layer, w_in, j, cos_tab, sin_tab):
    rows, d = x.shape
    qk = d // RET_HEADS
    tab_blocks = cos_tab.shape[0] // ROW_TILE
    row_spec = lambda width: pl.BlockSpec((ROW_TILE, width), lambda i: (i, 0))
    tab_spec = pl.BlockSpec((ROW_TILE, qk // 2), lambda i: (i % tab_blocks, 0))
    return pl.pallas_call(
        functools.partial(_ret_in_body, d=d, qk=qk),
        grid=(rows // ROW_TILE,),
        in_specs=[row_spec(d), _layer_spec(g.shape, layer), _layer_spec(w_in.shape, j), tab_spec, tab_spec],
        out_specs=[row_spec(d), row_spec(d), row_spec(2 * d), row_spec(2 * d)],
        out_shape=[jax.ShapeDtypeStruct((rows, d), BF16), jax.ShapeDtypeStruct((rows, d), BF16),
                   jax.ShapeDtypeStruct((rows, 2 * d), BF16), jax.ShapeDtypeStruct((rows, 2 * d), BF16)],
        compiler_params=_params(1),
        name="ret_in",
    )(x, g, w_in, cos_tab, sin_tab)


def _ret_core_body(q_ref, k_ref, v_ref, s0_ref, dmask_ref, qdec_ref, kdec_ref, cdec_ref,
                   o_ref, s_ref, *, lb, lp, qk, vd, seqs, chunks):
    c = pl.program_id(1)

    @pl.when(c == 0)
    def _():
        s_ref[...] = s0_ref[...]

    def padded(z):
        if lb == lp:
            return z
        return jnp.concatenate([z, jnp.zeros((lp - lb, z.shape[1]), z.dtype)], axis=0)

    for blk in range(seqs * chunks):
        b = blk // chunks
        rows = slice(blk * lb, (blk + 1) * lb)
        q, k, v = padded(q_ref[rows, :]), padded(k_ref[rows, :]), padded(v_ref[rows, :])
        for hd in range(RET_HEADS):
            qh = q[:, hd * qk:(hd + 1) * qk]
            kh = k[:, hd * qk:(hd + 1) * qk]
            kf = kh.astype(F32)
            vh = v[:, hd * vd:(hd + 1) * vd]
            s_old = s_ref[b, hd]
            scores = lax.dot_general(qh, kh, (((1,), (1,)), ((), ())), preferred_element_type=F32) * dmask_ref[hd]
            o = jnp.dot(scores.astype(BF16), vh, preferred_element_type=F32)
            o = o + qdec_ref[hd] * jnp.dot(qh, s_old.astype(BF16), preferred_element_type=F32)
            kd = (kf * kdec_ref[hd]).astype(BF16)
            s_ref[b, hd] = cdec_ref[hd] * s_old + lax.dot_general(kd, vh, (((0,), (0,)), ((), ())),
                                                                  preferred_element_type=F32)
            o_ref[rows, hd * vd:(hd + 1) * vd] = o[:lb]


def _ret_core(q, k, v, s0, consts, n, nc, lb):
    rows, d = q.shape
    qk, vd = d // RET_HEADS, 2 * d // RET_HEADS
    dmask, qdec, kdec, cdec = consts
    lp = dmask.shape[1]
    chunks = math.gcd(nc, max(ROW_TILE // lb, 1))
    seqs = math.gcd(n, RET_SEQS_PER_STEP) if chunks == nc else 1
    nc_blocks = nc // chunks
    row_spec = lambda width: pl.BlockSpec((seqs * chunks * lb, width), lambda b, c: (b * nc_blocks + c, 0))
    s_spec = pl.BlockSpec((seqs, RET_HEADS, qk, vd), lambda b, c: (b, 0, 0, 0))
    return pl.pallas_call(
        functools.partial(_ret_core_body, lb=lb, lp=lp, qk=qk, vd=vd, seqs=seqs, chunks=chunks),
        grid=(n // seqs, nc_blocks),
        in_specs=[row_spec(d), row_spec(d), row_spec(2 * d), s_spec,
                  _const_spec(dmask.shape), _const_spec(qdec.shape), _const_spec(kdec.shape), _const_spec(cdec.shape)],
        out_specs=[row_spec(2 * d), s_spec],
        out_shape=[jax.ShapeDtypeStruct((rows, 2 * d), F32), jax.ShapeDtypeStruct(s0.shape, F32)],
        compiler_params=_params(2),
        name="ret_core",
    )(q, k, v, s0, dmask, qdec, kdec, cdec)


def _ret_consts(l_real, l_pad, qk, vd):
    log_g = jnp.log1p(-jnp.exp2(-5.0 - jnp.arange(RET_HEADS, dtype=F32)))
    idx = jnp.arange(l_real, dtype=F32)
    diff = idx[:, None] - idx[None, :]
    dmask = jnp.exp(jnp.maximum(diff, 0.0)[None] * log_g[:, None, None]) * (diff >= 0)[None]
    q_dec = jnp.exp((idx + 1.0)[:, None] * log_g[None, :])
    k_dec = jnp.exp((l_real - 1.0 - idx)[:, None] * log_g[None, :])
    chunk_dec = jnp.exp(l_real * log_g)
    pad = l_pad - l_real
    dmask = jnp.pad(dmask, ((0, 0), (0, pad), (0, pad)))
    q_dec = jnp.pad(q_dec.T, ((0, 0), (0, pad)))
    k_dec = jnp.pad(k_dec.T, ((0, 0), (0, pad)))
    qdec = jnp.broadcast_to(q_dec[:, :, None], (RET_HEADS, l_pad, vd))
    kdec = jnp.broadcast_to(k_dec[:, :, None], (RET_HEADS, l_pad, qk))
    cdec = jnp.broadcast_to(chunk_dec[:, None, None], (RET_HEADS, 1, vd))
    return dmask, qdec, kdec, cdec


def _rope_tables(pos0, t, qk):
    half = qk // 2
    inv = ROPE_BASE ** (-jnp.arange(half, dtype=F32) / half)
    pos = pos0 + jnp.arange(t)
    ang = pos.astype(F32)[:, None] * inv[None, :]
    return jnp.cos(ang), jnp.sin(ang)


def _relayout(x, n, t, src, dst):
    if src == dst:
        return x
    lead = (n, t) if src == 'bm' else (t, n)
    return x.reshape(lead + (-1,)).transpose(1, 0, 2).reshape(n * t, -1)


def _mixer_layout(layer):
    return 'bm' if layer % 3 == 1 else 'tm'


def _trunk(x, pos0, st_re, st_im, st_ret, st_pool, wts):
    n, t, d = x.shape
    rows = n * t
    depth = wts['norm_mix'].shape[0]
    qk, vd = d // RET_HEADS, 2 * d // RET_HEADS
    tc = min(max(ROW_TILE // n, 1), t)
    in_kernel_relayout = _can_relayout_in_kernel(n, t)
    new_re, new_im, new_ret, new_pool = [], [], [], []

    xb, layout = x.reshape(rows, d), 'bm'
    for layer in range(depth):
        kind, j = layer % 3, layer // 3
        want = _mixer_layout(layer)
        if layout != want and not (kind == 0 and in_kernel_relayout):
            xb, layout = _relayout(xb, n, t, layout, want), want
        if kind == 1:
            l_real = math.gcd(t, RET_CHUNK)
            nc = t // l_real
            lb = max(l_real, BF16_ROWS)
            cos, sin = _rope_tables(pos0, t, qk)
            if t < ROW_TILE:
                cos, sin = jnp.tile(cos, (ROW_TILE // t, 1)), jnp.tile(sin, (ROW_TILE // t, 1))
            q, k, v, gate = _ret_in(xb, wts['norm_mix'], layer, wts['ret_w_in'], j, cos, sin)
            if lb != l_real:
                pad = lambda z: jnp.pad(z.reshape(n * nc, l_real, -1), ((0, 0), (0, lb - l_real), (0, 0))
                                        ).reshape(n * nc * lb, -1)
                q, k, v = pad(q), pad(k), pad(v)
            consts = _ret_consts(l_real, max(l_real, RET_MIN_CHUNK_ROWS), qk, vd)
            o, s_new = _ret_core(q, k, v, st_ret[j], consts, n, nc, lb)
            if lb != l_real:
                o = o.reshape(n * nc, lb, -1)[:, :l_real].reshape(rows, -1)
            proj, mixer = 'gated', (o, gate, wts['ret_w_out'], j)
            new_ret.append(s_new)
        elif kind == 0:
            gl, s_re, s_im = _s5_core(xb, layout, wts['norm_mix'], layer, wts['s5_packed'], wts['s5_d'], j,
                                      st_re[j].reshape(n, -1), st_im[j].reshape(n, -1), n, t, tc)
            proj, mixer = 'glu', (gl, wts['s5_w_glu'], j)
            new_re.append(s_re.reshape(st_re[j].shape))
            new_im.append(s_im.reshape(st_im[j].shape))
        else:
            buf = st_pool[j]
            buf_t = jnp.pad(buf.transpose(1, 0, 2), ((1, 0), (0, 0), (0, 0))).reshape(POOL_TAIL * n, d)
            if buf_t.size * buf_t.dtype.itemsize <= POOL_FUSE_MAX_TAIL_BYTES:
                proj, mixer = 'pool', (wts['norm_mix'], buf_t, wts['pool_w'], wts['pool_scale'], j)
            else:
                xb, tail = _pool(xb, wts['norm_mix'], layer, buf_t, wts['pool_w'], wts['pool_scale'], j, n, pos0)
                new_pool.append(tail.reshape(POOL_TAIL, n, d)[1:].transpose(1, 0, 2))
                proj, mixer = None, ()
        final = layer == depth - 1
        wanted = 'bm' if final else _mixer_layout(layer + 1)
        dst = wanted if in_kernel_relayout else layout
        xb, *tail = _mlp(proj, mixer, want, xb, layout, wts['norm_ffn'], wts['mlp_w_up'], wts['mlp_w_down'], layer,
                         wts['norm_final'], final, n, t, dst, pos0)
        if tail:
            new_pool.append(tail[0].reshape(POOL_TAIL, n, d)[1:].transpose(1, 0, 2))
        layout = dst
    y = _relayout(xb, n, t, layout, 'bm').reshape(n, t, d)
    return y, jnp.stack(new_re), jnp.stack(new_im), jnp.stack(new_ret), jnp.stack(new_pool)


def kernel(x_prompt, x_sample, state_s5_re, state_s5_im, state_ret, state_pool, norm_mix, norm_ffn, norm_final,
           s5_a_re, s5_a_im, s5_log_dt, s5_b_re, s5_b_im, s5_c_re, s5_c_im, s5_d, s5_w_glu, ret_w_in, ret_w_out,
           pool_w, pool_scale, mlp_w_up, mlp_w_down):
    nb = x_prompt.shape[0]
    depth, d = norm_mix.shape
    wts = dict(
        norm_mix=norm_mix.reshape(depth, 1, d), norm_ffn=norm_ffn.reshape(depth, 1, d), norm_final=norm_final,
        s5_d=s5_d.reshape(s5_d.shape[0], 1, d),
        s5_packed=jax.vmap(_s5_weights)(s5_a_re, s5_a_im, s5_log_dt, s5_b_re, s5_b_im, s5_c_re, s5_c_im),
        s5_w_glu=s5_w_glu.astype(BF16), ret_w_in=ret_w_in.astype(BF16), ret_w_out=ret_w_out.astype(BF16),
        pool_w=pool_w.astype(BF16), pool_scale=pool_scale.reshape(pool_scale.shape[0], 1, d),
        mlp_w_up=mlp_w_up.astype(BF16), mlp_w_down=mlp_w_down.astype(BF16))
    z_re = jnp.zeros((state_s5_re.shape[0], nb) + state_s5_re.shape[2:], state_s5_re.dtype)
    z_im = jnp.zeros((state_s5_im.shape[0], nb) + state_s5_im.shape[2:], state_s5_im.dtype)
    z_ret = jnp.zeros((state_ret.shape[0], nb) + state_ret.shape[2:], state_ret.dtype)
    z_pool = jnp.zeros((state_pool.shape[0], nb) + state_pool.shape[2:], state_pool.dtype)
    y_p, p_re, p_im, p_ret, p_pool = _trunk(x_prompt, 0, z_re, z_im, z_ret, z_pool, wts)
    y_s, s_re, s_im, s_ret, s_pool = _trunk(x_sample, PAST_LEN, state_s5_re, state_s5_im, state_ret, state_pool, wts)
    return (y_p, y_s, p_re, p_im, p_ret, p_pool, s_re, s_im, s_ret, s_pool)
```

```python
import functools
import math

import jax
import jax.numpy as jnp
from jax import lax
from jax.experimental import pallas as pl
from jax.experimental.pallas import tpu as pltpu

F32 = jnp.float32
BF16 = jnp.bfloat16

EPS = 1e-6
RET_HEADS = 4
RET_CHUNK = 256
RET_MIN_CHUNK_ROWS = 128
ROPE_BASE = 10000.0
POOL_WINDOWS = (2, 4, 8, 16)
POOL_TAIL = max(POOL_WINDOWS) - 1

SUBLANES = 8
BF16_ROWS = 16
MXU_DIM = 256
ROW_TILE = 512
SCAN_LANES = 512
RET_SEQS_PER_STEP = 4
POOL_FUSE_MAX_TAIL_BYTES = 1 << 20
VMEM_LIMIT = 56 * 1024 * 1024
PAST_LEN = 16384


def _params(n_axes):
    return pltpu.CompilerParams(dimension_semantics=("arbitrary",) * n_axes, vmem_limit_bytes=VMEM_LIMIT)


def _const_spec(shape):
    zeros = (0,) * len(shape)
    return pl.BlockSpec(shape, lambda *_: zeros, pipeline_mode=pl.Buffered(1))


def _layer_spec(stacked_shape, layer):
    idx = (layer,) + (0,) * (len(stacked_shape) - 1)
    return pl.BlockSpec((None,) + tuple(stacked_shape[1:]), lambda *_: idx, pipeline_mode=pl.Buffered(1))


def _can_relayout_in_kernel(n, t):
    steps = ROW_TILE // n
    return ROW_TILE % n == 0 and steps % BF16_ROWS == 0 and t % steps == 0


def _tile_spec(layout, mixed, n, width):
    if mixed and layout == 'bm':
        return pl.BlockSpec((n, ROW_TILE // n, width), lambda i: (0, i, 0))
    return pl.BlockSpec((ROW_TILE, width), lambda i: (i, 0))


def _tile_view(x, layout, mixed, n, t):
    return x.reshape(n, t, x.shape[-1]) if mixed and layout == 'bm' else x


def _permute_rows(y, n, to):
    rows, d = y.shape
    lead = (rows // n, n) if to == 'bm' else (n, rows // n)
    return jnp.swapaxes(y.reshape(lead + (d,)), 0, 1).reshape(rows, d)


def _rms(x, g):
    ms = jnp.mean(x * x, axis=-1, keepdims=True)
    return x * lax.rsqrt(ms + EPS) * g


def _gated_heads_proj(o, gate, wp_ref):
    vd = o.shape[1] // RET_HEADS
    mix = None
    for hd in range(RET_HEADS):
        cols = slice(hd * vd, (hd + 1) * vd)
        oc = o[:, cols] - jnp.mean(o[:, cols], axis=-1, keepdims=True)
        on = oc * lax.rsqrt(jnp.mean(oc * oc, axis=-1, keepdims=True) + EPS)
        og = (on * jax.nn.silu(gate[:, cols].astype(F32))).astype(BF16)
        part = jnp.dot(og, wp_ref[cols, :], preferred_element_type=F32)
        mix = part if mix is None else mix + part
    return mix


def _pool_mix(x, g_ref, buf_ref, w_ref, sc_ref, tail_out, tail, *, n, pos0):
    i = pl.program_id(0)

    @pl.when(i == 0)
    def _():
        tail[...] = buf_ref[...]

    rows, d = x.shape
    tc = rows // n
    h = _rms(x, g_ref[...])
    ext = jnp.concatenate([tail[...], h], axis=0)
    grp = d // len(POOL_WINDOWS)
    t_idx = i * tc + lax.shift_right_logical(lax.broadcasted_iota(jnp.int32, (rows, grp), 0), n.bit_length() - 1)
    zs = []
    for gi, w in enumerate(POOL_WINDOWS):
        cols = slice(gi * grp, (gi + 1) * grp)
        s, first, span = ext[:, cols], 0, 1
        while span < w:
            s = s[span * n:] + s[:-span * n]
            first += span
            span *= 2
        s = s[(POOL_TAIL - first) * n:]
        cnt = jnp.minimum(w, pos0 + t_idx + 1).astype(F32)
        pooled = s / cnt - h[:, cols]
        zs.append(jnp.dot(pooled.astype(BF16), w_ref[gi], preferred_element_type=F32))
    new_tail = ext[rows:]
    tail[...] = new_tail

    @pl.when(i == pl.num_programs(0) - 1)
    def _():
        tail_out[...] = new_tail

    return jnp.concatenate(zs, axis=1) * sc_ref[...]


MIXER_REFS = {None: 0, 'glu': 2, 'gated': 3, 'pool': 4}


def _mlp_body(*refs, proj, final, ff_chunk, n, mix_to, out_to, pos0):
    mix_refs = refs[:MIXER_REFS[proj]]
    x_ref, g_ref, wu_ref, wd_ref, gf_ref, o_ref = refs[len(mix_refs):len(mix_refs) + 6]
    d = x_ref.shape[-1]
    x = x_ref[...].reshape(ROW_TILE, d)
    if proj == 'pool':
        x = x + _pool_mix(x, *mix_refs, *refs[len(mix_refs) + 6:], n=n, pos0=pos0)
    elif proj is not None:
        a = mix_refs[0][...].reshape(ROW_TILE, mix_refs[0].shape[-1])
        if proj == 'glu':
            mix = jnp.dot(a, mix_refs[1][...], preferred_element_type=F32)
            mix = mix[:, :d] * jax.nn.sigmoid(mix[:, d:])
        else:
            mix = _gated_heads_proj(a, mix_refs[1][...].reshape(a.shape), mix_refs[2])
        if mix_to is not None:
            mix = _permute_rows(mix, n, mix_to)
        x = x + mix
    h = _rms(x, g_ref[...]).astype(BF16)
    acc = jnp.zeros_like(x)
    d_ff = wu_ref.shape[1]
    for c in range(d_ff // ff_chunk):
        cols = slice(c * ff_chunk, (c + 1) * ff_chunk)
        a = jnp.dot(h, wu_ref[:, cols], preferred_element_type=F32)
        a = jnp.square(jnp.maximum(a, 0.0)).astype(BF16)
        acc = acc + jnp.dot(a, wd_ref[cols, :], preferred_element_type=F32)
    y = x + acc
    if final:
        y = _rms(y, gf_ref[...])
    if out_to is not None:
        y = _permute_rows(y, n, out_to)
    o_ref[...] = y.reshape(o_ref.shape)


def _mlp(proj, mixer, a_layout, x, x_layout, g, w_up, w_down, layer, g_final, final, n, t, dst, pos0):
    d = x.shape[1]
    j = mixer[-1] if mixer else None
    mixed = len({x_layout, dst} | ({a_layout} if proj in ('glu', 'gated') else set())) > 1
    assert not mixed or _can_relayout_in_kernel(n, t)
    out_shape = [jax.ShapeDtypeStruct((n, t, d) if mixed and dst == 'bm' else (n * t, d), F32)]
    out_specs = [_tile_spec(dst, mixed, n, d)]
    scratch = []
    if proj == 'pool':
        assert x_layout == 'tm' and ROW_TILE % n == 0 and n & (n - 1) == 0
        g_mix, tail0, w_pool, scale = mixer[:-1]
        operands = [g_mix, tail0, w_pool, scale]
        specs = [_layer_spec(g_mix.shape, layer), _const_spec(tail0.shape), _layer_spec(w_pool.shape, j),
                 _layer_spec(scale.shape, j)]
        out_shape.append(jax.ShapeDtypeStruct(tail0.shape, F32))
        out_specs.append(pl.BlockSpec(tail0.shape, lambda i: (0, 0)))
        scratch.append(pltpu.VMEM(tail0.shape, F32))
    elif proj is None:
        operands, specs = [], []
    else:
        rows_in, w_proj = mixer[:-2], mixer[-2]
        operands = [_tile_view(z, a_layout, mixed, n, t) for z in rows_in] + [w_proj]
        specs = [_tile_spec(a_layout, mixed, n, z.shape[1]) for z in rows_in] + [_layer_spec(w_proj.shape, j)]
    operands += [_tile_view(x, x_layout, mixed, n, t), g, w_up, w_down, g_final.reshape(1, d)]
    specs += [_tile_spec(x_layout, mixed, n, d), _layer_spec(g.shape, layer), _layer_spec(w_up.shape, layer),
              _layer_spec(w_down.shape, layer), _const_spec((1, d))]
    outs = pl.pallas_call(
        functools.partial(_mlp_body, proj=proj, final=final, ff_chunk=1024, n=n, pos0=pos0,
                          mix_to=x_layout if proj in ('glu', 'gated') and a_layout != x_layout else None,
                          out_to=dst if dst != x_layout else None),
        grid=(n * t // ROW_TILE,),
        in_specs=specs,
        out_specs=out_specs,
        out_shape=out_shape,
        scratch_shapes=scratch,
        compiler_params=_params(1),
        name=(proj + "_mlp" if proj else "mlp") + ("_final" if final else ""),
    )(*operands)
    return (outs[0].reshape(n * t, d),) + tuple(outs[1:])


def _pool_body(g_ref, buf_ref, w_ref, sc_ref, x_ref, o_ref, tail_out, tail, *, n, pos0):
    x = x_ref[...]
    o_ref[...] = x + _pool_mix(x, g_ref, buf_ref, w_ref, sc_ref, tail_out, tail, n=n, pos0=pos0)


def _pool(x, g_mix, layer, tail0, w_pool, scale, j, n, pos0):
    rows, d = x.shape
    assert ROW_TILE % n == 0 and n & (n - 1) == 0
    return pl.pallas_call(
        functools.partial(_pool_body, n=n, pos0=pos0),
        grid=(rows // ROW_TILE,),
        in_specs=[_layer_spec(g_mix.shape, layer), _const_spec(tail0.shape), _layer_spec(w_pool.shape, j),
                  _layer_spec(scale.shape, j), pl.BlockSpec((ROW_TILE, d), lambda i: (i, 0))],
        out_specs=[pl.BlockSpec((ROW_TILE, d), lambda i: (i, 0)), pl.BlockSpec(tail0.shape, lambda i: (0, 0))],
        out_shape=[jax.ShapeDtypeStruct((rows, d), F32), jax.ShapeDtypeStruct(tail0.shape, F32)],
        scratch_shapes=[pltpu.VMEM(tail0.shape, F32)],
        compiler_params=_params(1),
        name="pool_mixer",
    )(g_mix, tail0, w_pool, scale, x)


def _s5_body(x_ref, g_ref, bw_ref, are_ref, aim_ref, cre_ref, cim_ref, dsk_ref, s0re_ref, s0im_ref,
             gl_ref, sre_out, sim_out, bu_re, bu_im, st_re, st_im, *, nb, tc):
    i = pl.program_id(0)

    @pl.when(i == 0)
    def _():
        st_re[...] = s0re_ref[...]
        st_im[...] = s0im_ref[...]

    d = x_ref.shape[-1]
    x = x_ref[...].reshape(tc * nb, d)
    if len(x_ref.shape) == 3:
        x = _permute_rows(x, nb, 'tm')
    h = _rms(x, g_ref[...])
    hb = h.astype(BF16)
    n_state_slab = bw_ref.shape[2] // 2
    ys = []
    for j in range(bw_ref.shape[0]):
        r = jnp.dot(hb[:, j * MXU_DIM:(j + 1) * MXU_DIM], bw_ref[j], preferred_element_type=F32)
        cols = slice(j * n_state_slab, (j + 1) * n_state_slab)
        bu_re[:, cols] = r[:, :n_state_slab]
        bu_im[:, cols] = r[:, n_state_slab:]
        for c in range(n_state_slab // SCAN_LANES):
            lanes = slice(j * n_state_slab + c * SCAN_LANES, j * n_state_slab + (c + 1) * SCAN_LANES)
            ar = jnp.broadcast_to(are_ref[:, lanes], (SUBLANES, SCAN_LANES))
            ai = jnp.broadcast_to(aim_ref[:, lanes], (SUBLANES, SCAN_LANES))
            for rg in range(nb // SUBLANES):
                seqs = slice(rg * SUBLANES, (rg + 1) * SUBLANES)
                sr, si = st_re[seqs, lanes], st_im[seqs, lanes]
                for t in range(tc):
                    rows = slice(t * nb + rg * SUBLANES, t * nb + (rg + 1) * SUBLANES)
                    sr, si = (ar * sr - ai * si + bu_re[rows, lanes], ar * si + ai * sr + bu_im[rows, lanes])
                    bu_re[rows, lanes] = sr
                    bu_im[rows, lanes] = si
                st_re[seqs, lanes] = sr
                st_im[seqs, lanes] = si
        y = jnp.dot(bu_re[:, cols].astype(BF16), cre_ref[j], preferred_element_type=F32)
        ys.append(y + jnp.dot(bu_im[:, cols].astype(BF16), cim_ref[j], preferred_element_type=F32))
    y = jnp.concatenate(ys, axis=1) + dsk_ref[...] * h
    gl_ref[...] = jax.nn.gelu(y, approximate=True).astype(BF16)

    @pl.when(i == pl.num_programs(0) - 1)
    def _():
        sre_out[...] = st_re[...]
        sim_out[...] = st_im[...]


def _s5_core(x, x_layout, g, layer, packed, d_skip, j, s0_re, s0_im, nb, t, tc):
    rows, d = x.shape
    bw, a_re, a_im, c_re, c_im = packed
    n_lanes = a_re.shape[-1]
    r = tc * nb
    from_bm = x_layout == 'bm'
    assert not from_bm or (r == ROW_TILE and _can_relayout_in_kernel(nb, t))
    return pl.pallas_call(
        functools.partial(_s5_body, nb=nb, tc=tc),
        grid=(rows // r,),
        in_specs=[_tile_spec(x_layout, from_bm, nb, d) if from_bm else pl.BlockSpec((r, d), lambda i: (i, 0)),
                  _layer_spec(g.shape, layer), _layer_spec(bw.shape, j), _layer_spec(a_re.shape, j),
                  _layer_spec(a_im.shape, j), _layer_spec(c_re.shape, j), _layer_spec(c_im.shape, j),
                  _layer_spec(d_skip.shape, j),
                  _const_spec((nb, n_lanes)), _const_spec((nb, n_lanes))],
        out_specs=[pl.BlockSpec((r, d), lambda i: (i, 0)),
                   pl.BlockSpec((nb, n_lanes), lambda i: (0, 0)), pl.BlockSpec((nb, n_lanes), lambda i: (0, 0))],
        out_shape=[jax.ShapeDtypeStruct((rows, d), BF16),
                   jax.ShapeDtypeStruct((nb, n_lanes), F32), jax.ShapeDtypeStruct((nb, n_lanes), F32)],
        scratch_shapes=[pltpu.VMEM((r, n_lanes), F32), pltpu.VMEM((r, n_lanes), F32),
                        pltpu.VMEM((nb, n_lanes), F32), pltpu.VMEM((nb, n_lanes), F32)],
        compiler_params=_params(1),
        name="s5_core",
    )(_tile_view(x, x_layout, from_bm, nb, t), g, bw, a_re, a_im, c_re, c_im, d_skip, s0_re, s0_im)


def _s5_weights(a_re, a_im, log_dt, b_re, b_im, c_re, c_im):
    groups, p, hch = b_re.shape
    dt = jnp.exp(log_dt.astype(F32))[:, None]
    lr, li = a_re.astype(F32), a_im.astype(F32)
    mag = jnp.exp(lr * dt)
    ab_re, ab_im = mag * jnp.cos(li * dt), mag * jnp.sin(li * dt)
    den = lr * lr + li * li
    nr, ni = ab_re - 1.0, ab_im
    f_re = (nr * lr + ni * li) / den
    f_im = (ni * lr - nr * li) / den
    br, bi = b_re.astype(F32), b_im.astype(F32)
    bb_re = f_re[..., None] * br - f_im[..., None] * bi
    bb_im = f_re[..., None] * bi + f_im[..., None] * br
    gs = MXU_DIM // hch
    n_slab = groups // gs

    def block_diag(per_group):
        r, c = per_group.shape[1:]
        tiled = jnp.tile(per_group.reshape(n_slab, gs * r, c), (1, 1, gs))
        row_g = lax.broadcasted_iota(jnp.int32, (gs * r, gs * c), 0) // r
        col_g = lax.broadcasted_iota(jnp.int32, (gs * r, gs * c), 1) // c
        return jnp.where(row_g == col_g, tiled, 0.0).astype(BF16)

    to_hp = lambda bb: bb.transpose(0, 2, 1)
    bw = jnp.concatenate([block_diag(to_hp(bb_re)), block_diag(to_hp(bb_im))], axis=2)
    cw_re = block_diag(c_re.astype(F32).transpose(0, 2, 1))
    cw_im = block_diag(-c_im.astype(F32).transpose(0, 2, 1))
    return bw, ab_re.reshape(1, groups * p), ab_im.reshape(1, groups * p), cw_re, cw_im


def _ret_in_body(x_ref, g_ref, w_ref, cos_ref, sin_ref, q_ref, k_ref, v_ref, gate_ref, *, d, qk):
    h = _rms(x_ref[...], g_ref[...]).astype(BF16)
    cos, sin = cos_ref[...], sin_ref[...]
    half = qk // 2

    def rot(z):
        outs = []
        for hd in range(d // qk):
            z1 = z[:, hd * qk:hd * qk + half]
            z2 = z[:, hd * qk + half:(hd + 1) * qk]
            outs += [z1 * cos - z2 * sin, z1 * sin + z2 * cos]
        return jnp.concatenate(outs, axis=1)

    q = jnp.dot(h, w_ref[:, 0:d], preferred_element_type=F32)
    q_ref[...] = (rot(q) * (qk ** -0.5)).astype(BF16)
    k = jnp.dot(h, w_ref[:, d:2 * d], preferred_element_type=F32)
    k_ref[...] = rot(k).astype(BF16)
    v_ref[...] = jnp.dot(h, w_ref[:, 2 * d:4 * d], preferred_element_type=F32).astype(BF16)
    gate_ref[...] = jnp.dot(h, w_ref[:, 4 * d:6 * d], preferred_element_type=F32).astype(BF16)


def _ret_in(x, g, layer, w_in, j, cos_tab, sin_tab):
    rows, d = x.shape
    qk = d // RET_HEADS
    tab_blocks = cos_tab.shape[0] // ROW_TILE
    row_spec = lambda width: pl.BlockSpec((ROW_TILE, width), lambda i: (i, 0))
    tab_spec = pl.BlockSpec((ROW_TILE, qk // 2), lambda i: (i % tab_blocks, 0))
    return pl.pallas_call(
        functools.partial(_ret_in_body, d=d, qk=qk),
        grid=(rows // ROW_TILE,),
        in_specs=[row_spec(d), _layer_spec(g.shape, layer), _layer_spec(w_in.shape, j), tab_spec, tab_spec],
        out_specs=[row_spec(d), row_spec(d), row_spec(2 * d), row_spec(2 * d)],
        out_shape=[jax.ShapeDtypeStruct((rows, d), BF16), jax.ShapeDtypeStruct((rows, d), BF16),
                   jax.ShapeDtypeStruct((rows, 2 * d), BF16), jax.ShapeDtypeStruct((rows, 2 * d), BF16)],
        compiler_params=_params(1),
        name="ret_in",
    )(x, g, w_in, cos_tab, sin_tab)


def _ret_core_body(q_ref, k_ref, v_ref, dmask_ref, qdec_ref, kdec_ref, cdec_ref, *rest,
                   lb, lp, qk, vd, seqs, chunks):
    o_ref, s_ref = rest[-2:]
    c = pl.program_id(1)

    @pl.when(c == 0)
    def _():
        s_ref[...] = rest[0][...] if len(rest) == 3 else jnp.zeros(s_ref.shape, s_ref.dtype)

    def padded(z):
        if lb == lp:
            return z
        return jnp.concatenate([z, jnp.zeros((lp - lb, z.shape[1]), z.dtype)], axis=0)

    for blk in range(seqs * chunks):
        b = blk // chunks
        rows = slice(blk * lb, (blk + 1) * lb)
        q, k, v = padded(q_ref[rows, :]), padded(k_ref[rows, :]), padded(v_ref[rows, :])
        for hd in range(RET_HEADS):
            qh = q[:, hd * qk:(hd + 1) * qk]
            kh = k[:, hd * qk:(hd + 1) * qk]
            kf = kh.astype(F32)
            vh = v[:, hd * vd:(hd + 1) * vd]
            s_old = s_ref[b, hd]
            scores = lax.dot_general(qh, kh, (((1,), (1,)), ((), ())), preferred_element_type=F32) * dmask_ref[hd]
            o = jnp.dot(scores.astype(BF16), vh, preferred_element_type=F32)
            o = o + qdec_ref[hd] * jnp.dot(qh, s_old.astype(BF16), preferred_element_type=F32)
            kd = (kf * kdec_ref[hd]).astype(BF16)
            s_ref[b, hd] = cdec_ref[hd] * s_old + lax.dot_general(kd, vh, (((0,), (0,)), ((), ())),
                                                                  preferred_element_type=F32)
            o_ref[rows, hd * vd:(hd + 1) * vd] = o[:lb]


def _ret_core(q, k, v, s0, consts, n, nc, lb):
    rows, d = q.shape
    qk, vd = d // RET_HEADS, 2 * d // RET_HEADS
    dmask, qdec, kdec, cdec = consts
    lp = dmask.shape[1]
    chunks = math.gcd(nc, max(ROW_TILE // lb, 1))
    seqs = math.gcd(n, RET_SEQS_PER_STEP) if chunks == nc else 1
    nc_blocks = nc // chunks
    row_spec = lambda width: pl.BlockSpec((seqs * chunks * lb, width), lambda b, c: (b * nc_blocks + c, 0))
    s_spec = pl.BlockSpec((seqs, RET_HEADS, qk, vd), lambda b, c: (b, 0, 0, 0))
    return pl.pallas_call(
        functools.partial(_ret_core_body, lb=lb, lp=lp, qk=qk, vd=vd, seqs=seqs, chunks=chunks),
        grid=(n // seqs, nc_blocks),
        in_specs=[row_spec(d), row_spec(d), row_spec(2 * d),
                  _const_spec(dmask.shape), _const_spec(qdec.shape), _const_spec(kdec.shape), _const_spec(cdec.shape)]
                 + ([] if s0 is None else [s_spec]),
        out_specs=[row_spec(2 * d), s_spec],
        out_shape=[jax.ShapeDtypeStruct((rows, 2 * d), F32), jax.ShapeDtypeStruct((n, RET_HEADS, qk, vd), F32)],
        compiler_params=_params(2),
        name="ret_core",
    )(q, k, v, dmask, qdec, kdec, cdec, *(() if s0 is None else (s0,)))


def _ret_consts(l_real, l_pad, qk, vd):
    log_g = jnp.log1p(-jnp.exp2(-5.0 - jnp.arange(RET_HEADS, dtype=F32)))
    idx = jnp.arange(l_real, dtype=F32)
    diff = idx[:, None] - idx[None, :]
    dmask = jnp.exp(jnp.maximum(diff, 0.0)[None] * log_g[:, None, None]) * (diff >= 0)[None]
    q_dec = jnp.exp((idx + 1.0)[:, None] * log_g[None, :])
    k_dec = jnp.exp((l_real - 1.0 - idx)[:, None] * log_g[None, :])
    chunk_dec = jnp.exp(l_real * log_g)
    pad = l_pad - l_real
    dmask = jnp.pad(dmask, ((0, 0), (0, pad), (0, pad)))
    q_dec = jnp.pad(q_dec.T, ((0, 0), (0, pad)))
    k_dec = jnp.pad(k_dec.T, ((0, 0), (0, pad)))
    qdec = jnp.broadcast_to(q_dec[:, :, None], (RET_HEADS, l_pad, vd))
    kdec = jnp.broadcast_to(k_dec[:, :, None], (RET_HEADS, l_pad, qk))
    cdec = jnp.broadcast_to(chunk_dec[:, None, None], (RET_HEADS, 1, vd))
    return dmask, qdec, kdec, cdec


def _rope_tables(pos0, t, qk):
    half = qk // 2
    inv = ROPE_BASE ** (-jnp.arange(half, dtype=F32) / half)
    pos = pos0 + jnp.arange(t)
    ang = pos.astype(F32)[:, None] * inv[None, :]
    return jnp.cos(ang), jnp.sin(ang)


def _relayout(x, n, t, src, dst):
    if src == dst:
        return x
    lead = (n, t) if src == 'bm' else (t, n)
    return x.reshape(lead + (-1,)).transpose(1, 0, 2).reshape(n * t, -1)


def _mixer_layout(layer):
    return 'bm' if layer % 3 == 1 else 'tm'


def _trunk(x, pos0, st_re, st_im, st_ret, st_pool, wts):
    n, t, d = x.shape
    rows = n * t
    depth = wts['norm_mix'].shape[0]
    qk, vd = d // RET_HEADS, 2 * d // RET_HEADS
    tc = min(max(ROW_TILE // n, 1), t)
    in_kernel_relayout = _can_relayout_in_kernel(n, t)
    new_re, new_im, new_ret, new_pool = [], [], [], []

    xb, layout = x.reshape(rows, d), 'bm'
    for layer in range(depth):
        kind, j = layer % 3, layer // 3
        want = _mixer_layout(layer)
        if layout != want and not (kind == 0 and in_kernel_relayout):
            xb, layout = _relayout(xb, n, t, layout, want), want
        if kind == 1:
            l_real = math.gcd(t, RET_CHUNK)
            nc = t // l_real
            lb = max(l_real, BF16_ROWS)
            cos, sin = _rope_tables(pos0, t, qk)
            if t < ROW_TILE:
                cos, sin = jnp.tile(cos, (ROW_TILE // t, 1)), jnp.tile(sin, (ROW_TILE // t, 1))
            q, k, v, gate = _ret_in(xb, wts['norm_mix'], layer, wts['ret_w_in'], j, cos, sin)
            if lb != l_real:
                pad = lambda z: jnp.pad(z.reshape(n * nc, l_real, -1), ((0, 0), (0, lb - l_real), (0, 0))
                                        ).reshape(n * nc * lb, -1)
                q, k, v = pad(q), pad(k), pad(v)
            consts = _ret_consts(l_real, max(l_real, RET_MIN_CHUNK_ROWS), qk, vd)
            o, s_new = _ret_core(q, k, v, None if st_ret is None else st_ret[j], consts, n, nc, lb)
            if lb != l_real:
                o = o.reshape(n * nc, lb, -1)[:, :l_real].reshape(rows, -1)
            proj, mixer = 'gated', (o, gate, wts['ret_w_out'], j)
            new_ret.append(s_new)
        elif kind == 0:
            gl, s_re, s_im = _s5_core(xb, layout, wts['norm_mix'], layer, wts['s5_packed'], wts['s5_d'], j,
                                      st_re[j].reshape(n, -1), st_im[j].reshape(n, -1), n, t, tc)
            proj, mixer = 'glu', (gl, wts['s5_w_glu'], j)
            new_re.append(s_re.reshape(st_re[j].shape))
            new_im.append(s_im.reshape(st_im[j].shape))
        else:
            buf = st_pool[j]
            buf_t = buf.transpose(1, 0, 2).reshape(POOL_TAIL * n, d)
            if buf_t.size * buf_t.dtype.itemsize <= POOL_FUSE_MAX_TAIL_BYTES:
                proj, mixer = 'pool', (wts['norm_mix'], buf_t, wts['pool_w'], wts['pool_scale'], j)
            else:
                xb, tail = _pool(xb, wts['norm_mix'], layer, buf_t, wts['pool_w'], wts['pool_scale'], j, n, pos0)
                new_pool.append(tail.reshape(POOL_TAIL, n, d).transpose(1, 0, 2))
                proj, mixer = None, ()
        final = layer == depth - 1
        wanted = 'bm' if final else _mixer_layout(layer + 1)
        dst = wanted if in_kernel_relayout else layout
        xb, *tail = _mlp(proj, mixer, want, xb, layout, wts['norm_ffn'], wts['mlp_w_up'], wts['mlp_w_down'], layer,
                         wts['norm_final'], final, n, t, dst, pos0)
        if tail:
            new_pool.append(tail[0].reshape(POOL_TAIL, n, d).transpose(1, 0, 2))
        layout = dst
    y = _relayout(xb, n, t, layout, 'bm').reshape(n, t, d)
    return y, jnp.stack(new_re), jnp.stack(new_im), jnp.stack(new_ret), jnp.stack(new_pool)


def kernel(x_prompt, x_sample, state_s5_re, state_s5_im, state_ret, state_pool, norm_mix, norm_ffn, norm_final,
           s5_a_re, s5_a_im, s5_log_dt, s5_b_re, s5_b_im, s5_c_re, s5_c_im, s5_d, s5_w_glu, ret_w_in, ret_w_out,
           pool_w, pool_scale, mlp_w_up, mlp_w_down):
    nb = x_prompt.shape[0]
    depth, d = norm_mix.shape
    wts = dict(
        norm_mix=norm_mix.reshape(depth, 1, d), norm_ffn=norm_ffn.reshape(depth, 1, d), norm_final=norm_final,
        s5_d=s5_d.reshape(s5_d.shape[0], 1, d),
        s5_packed=jax.vmap(_s5_weights)(s5_a_re, s5_a_im, s5_log_dt, s5_b_re, s5_b_im, s5_c_re, s5_c_im),
        s5_w_glu=s5_w_glu.astype(BF16), ret_w_in=ret_w_in.astype(BF16), ret_w_out=ret_w_out.astype(BF16),
        pool_w=pool_w.astype(BF16), pool_scale=pool_scale.reshape(pool_scale.shape[0], 1, d),
        mlp_w_up=mlp_w_up.astype(BF16), mlp_w_down=mlp_w_down.astype(BF16))
    z_re = jnp.zeros((state_s5_re.shape[0], nb) + state_s5_re.shape[2:], state_s5_re.dtype)
    z_im = jnp.zeros((state_s5_im.shape[0], nb) + state_s5_im.shape[2:], state_s5_im.dtype)
    z_pool = jnp.zeros((state_pool.shape[0], nb) + state_pool.shape[2:], state_pool.dtype)
    y_p, p_re, p_im, p_ret, p_pool = _trunk(x_prompt, 0, z_re, z_im, None, z_pool, wts)
    y_s, s_re, s_im, s_ret, s_pool = _trunk(x_sample, PAST_LEN, state_s5_re, state_s5_im, state_ret, state_pool, wts)
    return (y_p, y_s, p_re, p_im, p_ret, p_pool, s_re, s_im, s_ret, s_pool)
```

```python
import functools
import math

import jax
import jax.numpy as jnp
from jax import lax
from jax.experimental import pallas as pl
from jax.experimental.pallas import tpu as pltpu

F32 = jnp.float32
BF16 = jnp.bfloat16

EPS = 1e-6
RET_HEADS = 4
RET_CHUNK = 256
RET_MIN_CHUNK_ROWS = 128
ROPE_BASE = 10000.0
POOL_WINDOWS = (2, 4, 8, 16)
POOL_TAIL = max(POOL_WINDOWS) - 1

SUBLANES = 8
BF16_ROWS = 16
MXU_DIM = 256
ROW_TILE = 512
SCAN_LANES = 512
RET_SEQS_PER_STEP = 4
POOL_FUSE_MAX_TAIL_BYTES = 1 << 20
VMEM_LIMIT = 56 * 1024 * 1024
PAST_LEN = 16384


def _params(n_axes):
    return pltpu.CompilerParams(dimension_semantics=("arbitrary",) * n_axes, vmem_limit_bytes=VMEM_LIMIT)


def _const_spec(shape):
    zeros = (0,) * len(shape)
    return pl.BlockSpec(shape, lambda *_: zeros, pipeline_mode=pl.Buffered(1))


def _layer_spec(stacked_shape, layer):
    idx = (layer,) + (0,) * (len(stacked_shape) - 1)
    return pl.BlockSpec((None,) + tuple(stacked_shape[1:]), lambda *_: idx, pipeline_mode=pl.Buffered(1))


def _can_relayout_in_kernel(n, t):
    steps = ROW_TILE // n
    return ROW_TILE % n == 0 and steps % BF16_ROWS == 0 and t % steps == 0


def _tile_spec(layout, mixed, n, width):
    if mixed and layout == 'bm':
        return pl.BlockSpec((n, ROW_TILE // n, width), lambda i: (0, i, 0))
    return pl.BlockSpec((ROW_TILE, width), lambda i: (i, 0))


def _tile_view(x, layout, mixed, n, t):
    return x.reshape(n, t, x.shape[-1]) if mixed and layout == 'bm' else x


def _permute_rows(y, n, to):
    rows, d = y.shape
    lead = (rows // n, n) if to == 'bm' else (n, rows // n)
    return jnp.swapaxes(y.reshape(lead + (d,)), 0, 1).reshape(rows, d)


def _rms(x, g):
    ms = jnp.mean(x * x, axis=-1, keepdims=True)
    return x * lax.rsqrt(ms + EPS) * g


def _gated_heads_proj(o, gate, wp_ref):
    vd = o.shape[1] // RET_HEADS
    mix = None
    for hd in range(RET_HEADS):
        cols = slice(hd * vd, (hd + 1) * vd)
        oc = o[:, cols] - jnp.mean(o[:, cols], axis=-1, keepdims=True)
        on = oc * lax.rsqrt(jnp.mean(oc * oc, axis=-1, keepdims=True) + EPS)
        og = (on * jax.nn.silu(gate[:, cols].astype(F32))).astype(BF16)
        part = jnp.dot(og, wp_ref[cols, :], preferred_element_type=F32)
        mix = part if mix is None else mix + part
    return mix


def _pool_mix(x, g_ref, buf_ref, w_ref, sc_ref, tail_out, tail, *, n, pos0):
    i = pl.program_id(0)

    @pl.when(i == 0)
    def _():
        tail[...] = buf_ref[...]

    rows, d = x.shape
    tc = rows // n
    h = _rms(x, g_ref[...])
    ext = jnp.concatenate([tail[...], h], axis=0)
    grp = d // len(POOL_WINDOWS)
    t_idx = i * tc + lax.shift_right_logical(lax.broadcasted_iota(jnp.int32, (rows, grp), 0), n.bit_length() - 1)
    zs = []
    for gi, w in enumerate(POOL_WINDOWS):
        cols = slice(gi * grp, (gi + 1) * grp)
        s, first, span = ext[:, cols], 0, 1
        while span < w:
            s = s[span * n:] + s[:-span * n]
            first += span
            span *= 2
        s = s[(POOL_TAIL - first) * n:]
        cnt = jnp.minimum(w, pos0 + t_idx + 1).astype(F32)
        pooled = s / cnt - h[:, cols]
        zs.append(jnp.dot(pooled.astype(BF16), w_ref[gi], preferred_element_type=F32))
    new_tail = ext[rows:]
    tail[...] = new_tail

    @pl.when(i == pl.num_programs(0) - 1)
    def _():
        tail_out[...] = new_tail

    return jnp.concatenate(zs, axis=1) * sc_ref[...]


MIXER_REFS = {None: 0, 'glu': 2, 'gated': 3, 'pool': 4}


def _mlp_body(*refs, proj, final, ff_chunk, n, mix_to, out_to, pos0):
    mix_refs = refs[:MIXER_REFS[proj]]
    x_ref, g_ref, wu_ref, wd_ref, gf_ref, o_ref = refs[len(mix_refs):len(mix_refs) + 6]
    d = x_ref.shape[-1]
    x = x_ref[...].reshape(ROW_TILE, d)
    if proj == 'pool':
        x = x + _pool_mix(x, *mix_refs, *refs[len(mix_refs) + 6:], n=n, pos0=pos0)
    elif proj is not None:
        a = mix_refs[0][...].reshape(ROW_TILE, mix_refs[0].shape[-1])
        if proj == 'glu':
            mix = jnp.dot(a, mix_refs[1][...], preferred_element_type=F32)
            mix = mix[:, :d] * jax.nn.sigmoid(mix[:, d:])
        else:
            mix = _gated_heads_proj(a, mix_refs[1][...].reshape(a.shape), mix_refs[2])
        if mix_to is not None:
            mix = _permute_rows(mix, n, mix_to)
        x = x + mix
    h = _rms(x, g_ref[...]).astype(BF16)
    acc = jnp.zeros_like(x)
    d_ff = wu_ref.shape[1]
    for c in range(d_ff // ff_chunk):
        cols = slice(c * ff_chunk, (c + 1) * ff_chunk)
        a = jnp.dot(h, wu_ref[:, cols], preferred_element_type=F32)
        a = jnp.square(jnp.maximum(a, 0.0)).astype(BF16)
        acc = acc + jnp.dot(a, wd_ref[cols, :], preferred_element_type=F32)
    y = x + acc
    if final:
        y = _rms(y, gf_ref[...])
    if out_to is not None:
        y = _permute_rows(y, n, out_to)
    o_ref[...] = y.reshape(o_ref.shape)


def _mlp(proj, mixer, a_layout, x, x_layout, g, w_up, w_down, layer, g_final, final, n, t, dst, pos0):
    d = x.shape[1]
    j = mixer[-1] if mixer else None
    mixed = len({x_layout, dst} | ({a_layout} if proj in ('glu', 'gated') else set())) > 1
    assert not mixed or _can_relayout_in_kernel(n, t)
    out_shape = [jax.ShapeDtypeStruct((n, t, d) if mixed and dst == 'bm' else (n * t, d), F32)]
    out_specs = [_tile_spec(dst, mixed, n, d)]
    scratch = []
    if proj == 'pool':
        assert x_layout == 'tm' and ROW_TILE % n == 0 and n & (n - 1) == 0
        g_mix, tail0, w_pool, scale = mixer[:-1]
        operands = [g_mix, tail0, w_pool, scale]
        specs = [_layer_spec(g_mix.shape, layer), _const_spec(tail0.shape), _layer_spec(w_pool.shape, j),
                 _layer_spec(scale.shape, j)]
        out_shape.append(jax.ShapeDtypeStruct(tail0.shape, F32))
        out_specs.append(pl.BlockSpec(tail0.shape, lambda i: (0, 0)))
        scratch.append(pltpu.VMEM(tail0.shape, F32))
    elif proj is None:
        operands, specs = [], []
    else:
        rows_in, w_proj = mixer[:-2], mixer[-2]
        operands = [_tile_view(z, a_layout, mixed, n, t) for z in rows_in] + [w_proj]
        specs = [_tile_spec(a_layout, mixed, n, z.shape[1]) for z in rows_in] + [_layer_spec(w_proj.shape, j)]
    operands += [_tile_view(x, x_layout, mixed, n, t), g, w_up, w_down, g_final.reshape(1, d)]
    specs += [_tile_spec(x_layout, mixed, n, d), _layer_spec(g.shape, layer), _const_spec(w_up.shape),
              _const_spec(w_down.shape), _const_spec((1, d))]
    outs = pl.pallas_call(
        functools.partial(_mlp_body, proj=proj, final=final, ff_chunk=1024, n=n, pos0=pos0,
                          mix_to=x_layout if proj in ('glu', 'gated') and a_layout != x_layout else None,
                          out_to=dst if dst != x_layout else None),
        grid=(n * t // ROW_TILE,),
        in_specs=specs,
        out_specs=out_specs,
        out_shape=out_shape,
        scratch_shapes=scratch,
        compiler_params=_params(1),
        name=(proj + "_mlp" if proj else "mlp") + ("_final" if final else ""),
    )(*operands)
    return (outs[0].reshape(n * t, d),) + tuple(outs[1:])


def _pool_body(g_ref, buf_ref, w_ref, sc_ref, x_ref, o_ref, tail_out, tail, *, n, pos0):
    x = x_ref[...]
    o_ref[...] = x + _pool_mix(x, g_ref, buf_ref, w_ref, sc_ref, tail_out, tail, n=n, pos0=pos0)


def _pool(x, g_mix, layer, tail0, w_pool, scale, j, n, pos0):
    rows, d = x.shape
    assert ROW_TILE % n == 0 and n & (n - 1) == 0
    return pl.pallas_call(
        functools.partial(_pool_body, n=n, pos0=pos0),
        grid=(rows // ROW_TILE,),
        in_specs=[_layer_spec(g_mix.shape, layer), _const_spec(tail0.shape), _layer_spec(w_pool.shape, j),
                  _layer_spec(scale.shape, j), pl.BlockSpec((ROW_TILE, d), lambda i: (i, 0))],
        out_specs=[pl.BlockSpec((ROW_TILE, d), lambda i: (i, 0)), pl.BlockSpec(tail0.shape, lambda i: (0, 0))],
        out_shape=[jax.ShapeDtypeStruct((rows, d), F32), jax.ShapeDtypeStruct(tail0.shape, F32)],
        scratch_shapes=[pltpu.VMEM(tail0.shape, F32)],
        compiler_params=_params(1),
        name="pool_mixer",
    )(g_mix, tail0, w_pool, scale, x)


def _cast_specs(jobs, steps):
    operands, in_specs, out_specs, out_shapes = [], [], [], []
    for w, layer in jobs:
        _, rows, cols = w.shape
        assert rows % (steps * BF16_ROWS) == 0
        blk = rows // steps
        operands.append(w)
        in_specs.append(pl.BlockSpec((None, blk, cols), lambda i, layer=layer: (layer, i, 0)))
        out_specs.append(pl.BlockSpec((blk, cols), lambda i: (i, 0)))
        out_shapes.append(jax.ShapeDtypeStruct((rows, cols), BF16))
    return operands, in_specs, out_specs, out_shapes


def _run_casts(src_refs, dst_refs):
    for src, dst in zip(src_refs, dst_refs, strict=True):
        dst[...] = src[...].astype(BF16)


def _s5_body(*refs, nb, tc, n_cast):
    x_ref, g_ref, bw_ref, are_ref, aim_ref, cre_ref, cim_ref, dsk_ref, s0re_ref, s0im_ref = refs[:10]
    cast_src, refs = refs[10:10 + n_cast], refs[10 + n_cast:]
    gl_ref, sre_out, sim_out = refs[:3]
    cast_dst, (bu_re, bu_im, st_re, st_im) = refs[3:3 + n_cast], refs[3 + n_cast:]
    _run_casts(cast_src, cast_dst)
    i = pl.program_id(0)

    @pl.when(i == 0)
    def _():
        st_re[...] = s0re_ref[...]
        st_im[...] = s0im_ref[...]

    d = x_ref.shape[-1]
    x = x_ref[...].reshape(tc * nb, d)
    if len(x_ref.shape) == 3:
        x = _permute_rows(x, nb, 'tm')
    h = _rms(x, g_ref[...])
    hb = h.astype(BF16)
    n_state_slab = bw_ref.shape[2] // 2
    ys = []
    for j in range(bw_ref.shape[0]):
        r = jnp.dot(hb[:, j * MXU_DIM:(j + 1) * MXU_DIM], bw_ref[j], preferred_element_type=F32)
        cols = slice(j * n_state_slab, (j + 1) * n_state_slab)
        bu_re[:, cols] = r[:, :n_state_slab]
        bu_im[:, cols] = r[:, n_state_slab:]
        for c in range(n_state_slab // SCAN_LANES):
            lanes = slice(j * n_state_slab + c * SCAN_LANES, j * n_state_slab + (c + 1) * SCAN_LANES)
            ar = jnp.broadcast_to(are_ref[:, lanes], (SUBLANES, SCAN_LANES))
            ai = jnp.broadcast_to(aim_ref[:, lanes], (SUBLANES, SCAN_LANES))
            for rg in range(nb // SUBLANES):
                seqs = slice(rg * SUBLANES, (rg + 1) * SUBLANES)
                sr, si = st_re[seqs, lanes], st_im[seqs, lanes]
                for t in range(tc):
                    rows = slice(t * nb + rg * SUBLANES, t * nb + (rg + 1) * SUBLANES)
                    sr, si = (ar * sr - ai * si + bu_re[rows, lanes], ar * si + ai * sr + bu_im[rows, lanes])
                    bu_re[rows, lanes] = sr
                    bu_im[rows, lanes] = si
                st_re[seqs, lanes] = sr
                st_im[seqs, lanes] = si
        y = jnp.dot(bu_re[:, cols].astype(BF16), cre_ref[j], preferred_element_type=F32)
        ys.append(y + jnp.dot(bu_im[:, cols].astype(BF16), cim_ref[j], preferred_element_type=F32))
    y = jnp.concatenate(ys, axis=1) + dsk_ref[...] * h
    gl_ref[...] = jax.nn.gelu(y, approximate=True).astype(BF16)

    @pl.when(i == pl.num_programs(0) - 1)
    def _():
        sre_out[...] = st_re[...]
        sim_out[...] = st_im[...]


def _s5_core(x, x_layout, g, layer, packed, d_skip, j, s0_re, s0_im, nb, t, tc, cast_jobs):
    rows, d = x.shape
    bw, a_re, a_im, c_re, c_im = packed
    n_lanes = a_re.shape[-1]
    r = tc * nb
    from_bm = x_layout == 'bm'
    assert not from_bm or (r == ROW_TILE and _can_relayout_in_kernel(nb, t))
    cast_ops, cast_in, cast_out, cast_shapes = _cast_specs(cast_jobs, rows // r)
    return pl.pallas_call(
        functools.partial(_s5_body, nb=nb, tc=tc, n_cast=len(cast_jobs)),
        grid=(rows // r,),
        in_specs=[_tile_spec(x_layout, from_bm, nb, d) if from_bm else pl.BlockSpec((r, d), lambda i: (i, 0)),
                  _layer_spec(g.shape, layer), _layer_spec(bw.shape, j), _layer_spec(a_re.shape, j),
                  _layer_spec(a_im.shape, j), _layer_spec(c_re.shape, j), _layer_spec(c_im.shape, j),
                  _layer_spec(d_skip.shape, j),
                  _const_spec((nb, n_lanes)), _const_spec((nb, n_lanes))] + cast_in,
        out_specs=[pl.BlockSpec((r, d), lambda i: (i, 0)),
                   pl.BlockSpec((nb, n_lanes), lambda i: (0, 0)), pl.BlockSpec((nb, n_lanes), lambda i: (0, 0))]
                  + cast_out,
        out_shape=[jax.ShapeDtypeStruct((rows, d), BF16),
                   jax.ShapeDtypeStruct((nb, n_lanes), F32), jax.ShapeDtypeStruct((nb, n_lanes), F32)] + cast_shapes,
        scratch_shapes=[pltpu.VMEM((r, n_lanes), F32), pltpu.VMEM((r, n_lanes), F32),
                        pltpu.VMEM((nb, n_lanes), F32), pltpu.VMEM((nb, n_lanes), F32)],
        compiler_params=_params(1),
        name="s5_core",
    )(_tile_view(x, x_layout, from_bm, nb, t), g, bw, a_re, a_im, c_re, c_im, d_skip, s0_re, s0_im, *cast_ops)


def _s5_weights(a_re, a_im, log_dt, b_re, b_im, c_re, c_im):
    groups, p, hch = b_re.shape
    dt = jnp.exp(log_dt.astype(F32))[:, None]
    lr, li = a_re.astype(F32), a_im.astype(F32)
    mag = jnp.exp(lr * dt)
    ab_re, ab_im = mag * jnp.cos(li * dt), mag * jnp.sin(li * dt)
    den = lr * lr + li * li
    nr, ni = ab_re - 1.0, ab_im
    f_re = (nr * lr + ni * li) / den
    f_im = (ni * lr - nr * li) / den
    br, bi = b_re.astype(F32), b_im.astype(F32)
    bb_re = f_re[..., None] * br - f_im[..., None] * bi
    bb_im = f_re[..., None] * bi + f_im[..., None] * br
    gs = MXU_DIM // hch
    n_slab = groups // gs

    def block_diag(per_group):
        r, c = per_group.shape[1:]
        tiled = jnp.tile(per_group.reshape(n_slab, gs * r, c), (1, 1, gs))
        row_g = lax.broadcasted_iota(jnp.int32, (gs * r, gs * c), 0) // r
        col_g = lax.broadcasted_iota(jnp.int32, (gs * r, gs * c), 1) // c
        return jnp.where(row_g == col_g, tiled, 0.0).astype(BF16)

    to_hp = lambda bb: bb.transpose(0, 2, 1)
    bw = jnp.concatenate([block_diag(to_hp(bb_re)), block_diag(to_hp(bb_im))], axis=2)
    cw_re = block_diag(c_re.astype(F32).transpose(0, 2, 1))
    cw_im = block_diag(-c_im.astype(F32).transpose(0, 2, 1))
    return bw, ab_re.reshape(1, groups * p), ab_im.reshape(1, groups * p), cw_re, cw_im


def _ret_in_body(*refs, d, qk, n_cast):
    x_ref, g_ref, w_ref, cos_ref, sin_ref = refs[:5]
    q_ref, k_ref, v_ref, gate_ref = refs[5 + n_cast:9 + n_cast]
    _run_casts(refs[5:5 + n_cast], refs[9 + n_cast:])
    h = _rms(x_ref[...], g_ref[...]).astype(BF16)
    cos, sin = cos_ref[...], sin_ref[...]
    half = qk // 2

    def rot(z):
        outs = []
        for hd in range(d // qk):
            z1 = z[:, hd * qk:hd * qk + half]
            z2 = z[:, hd * qk + half:(hd + 1) * qk]
            outs += [z1 * cos - z2 * sin, z1 * sin + z2 * cos]
        return jnp.concatenate(outs, axis=1)

    q = jnp.dot(h, w_ref[:, 0:d], preferred_element_type=F32)
    q_ref[...] = (rot(q) * (qk ** -0.5)).astype(BF16)
    k = jnp.dot(h, w_ref[:, d:2 * d], preferred_element_type=F32)
    k_ref[...] = rot(k).astype(BF16)
    v_ref[...] = jnp.dot(h, w_ref[:, 2 * d:4 * d], preferred_element_type=F32).astype(BF16)
    gate_ref[...] = jnp.dot(h, w_ref[:, 4 * d:6 * d], preferred_element_type=F32).astype(BF16)


def _ret_in(x, g, layer, w_in, j, cos_tab, sin_tab, cast_jobs):
    rows, d = x.shape
    qk = d // RET_HEADS
    tab_blocks = cos_tab.shape[0] // ROW_TILE
    row_spec = lambda width: pl.BlockSpec((ROW_TILE, width), lambda i: (i, 0))
    tab_spec = pl.BlockSpec((ROW_TILE, qk // 2), lambda i: (i % tab_blocks, 0))
    cast_ops, cast_in, cast_out, cast_shapes = _cast_specs(cast_jobs, rows // ROW_TILE)
    return pl.pallas_call(
        functools.partial(_ret_in_body, d=d, qk=qk, n_cast=len(cast_jobs)),
        grid=(rows // ROW_TILE,),
        in_specs=[row_spec(d), _layer_spec(g.shape, layer), _layer_spec(w_in.shape, j), tab_spec, tab_spec] + cast_in,
        out_specs=[row_spec(d), row_spec(d), row_spec(2 * d), row_spec(2 * d)] + cast_out,
        out_shape=[jax.ShapeDtypeStruct((rows, d), BF16), jax.ShapeDtypeStruct((rows, d), BF16),
                   jax.ShapeDtypeStruct((rows, 2 * d), BF16), jax.ShapeDtypeStruct((rows, 2 * d), BF16)] + cast_shapes,
        compiler_params=_params(1),
        name="ret_in",
    )(x, g, w_in, cos_tab, sin_tab, *cast_ops)


def _ret_core_body(q_ref, k_ref, v_ref, dmask_ref, qdec_ref, kdec_ref, cdec_ref, *rest,
                   lb, lp, qk, vd, seqs, chunks):
    o_ref, s_ref = rest[-2:]
    c = pl.program_id(1)

    @pl.when(c == 0)
    def _():
        s_ref[...] = rest[0][...] if len(rest) == 3 else jnp.zeros(s_ref.shape, s_ref.dtype)

    def padded(z):
        if lb == lp:
            return z
        return jnp.concatenate([z, jnp.zeros((lp - lb, z.shape[1]), z.dtype)], axis=0)

    for blk in range(seqs * chunks):
        b = blk // chunks
        rows = slice(blk * lb, (blk + 1) * lb)
        q, k, v = padded(q_ref[rows, :]), padded(k_ref[rows, :]), padded(v_ref[rows, :])
        for hd in range(RET_HEADS):
            qh = q[:, hd * qk:(hd + 1) * qk]
            kh = k[:, hd * qk:(hd + 1) * qk]
            kf = kh.astype(F32)
            vh = v[:, hd * vd:(hd + 1) * vd]
            s_old = s_ref[b, hd]
            scores = lax.dot_general(qh, kh, (((1,), (1,)), ((), ())), preferred_element_type=F32) * dmask_ref[hd]
            o = jnp.dot(scores.astype(BF16), vh, preferred_element_type=F32)
            o = o + qdec_ref[hd] * jnp.dot(qh, s_old.astype(BF16), preferred_element_type=F32)
            kd = (kf * kdec_ref[hd]).astype(BF16)
            s_ref[b, hd] = cdec_ref[hd] * s_old + lax.dot_general(kd, vh, (((0,), (0,)), ((), ())),
                                                                  preferred_element_type=F32)
            o_ref[rows, hd * vd:(hd + 1) * vd] = o[:lb]


def _ret_core(q, k, v, s0, consts, n, nc, lb):
    rows, d = q.shape
    qk, vd = d // RET_HEADS, 2 * d // RET_HEADS
    dmask, qdec, kdec, cdec = consts
    lp = dmask.shape[1]
    chunks = math.gcd(nc, max(ROW_TILE // lb, 1))
    seqs = math.gcd(n, RET_SEQS_PER_STEP) if chunks == nc else 1
    nc_blocks = nc // chunks
    row_spec = lambda width: pl.BlockSpec((seqs * chunks * lb, width), lambda b, c: (b * nc_blocks + c, 0))
    s_spec = pl.BlockSpec((seqs, RET_HEADS, qk, vd), lambda b, c: (b, 0, 0, 0))
    return pl.pallas_call(
        functools.partial(_ret_core_body, lb=lb, lp=lp, qk=qk, vd=vd, seqs=seqs, chunks=chunks),
        grid=(n // seqs, nc_blocks),
        in_specs=[row_spec(d), row_spec(d), row_spec(2 * d),
                  _const_spec(dmask.shape), _const_spec(qdec.shape), _const_spec(kdec.shape), _const_spec(cdec.shape)]
                 + ([] if s0 is None else [s_spec]),
        out_specs=[row_spec(2 * d), s_spec],
        out_shape=[jax.ShapeDtypeStruct((rows, 2 * d), F32), jax.ShapeDtypeStruct((n, RET_HEADS, qk, vd), F32)],
        compiler_params=_params(2),
        name="ret_core",
    )(q, k, v, dmask, qdec, kdec, cdec, *(() if s0 is None else (s0,)))


def _ret_consts(l_real, l_pad, qk, vd):
    log_g = jnp.log1p(-jnp.exp2(-5.0 - jnp.arange(RET_HEADS, dtype=F32)))
    idx = jnp.arange(l_real, dtype=F32)
    diff = idx[:, None] - idx[None, :]
    dmask = jnp.exp(jnp.maximum(diff, 0.0)[None] * log_g[:, None, None]) * (diff >= 0)[None]
    q_dec = jnp.exp((idx + 1.0)[:, None] * log_g[None, :])
    k_dec = jnp.exp((l_real - 1.0 - idx)[:, None] * log_g[None, :])
    chunk_dec = jnp.exp(l_real * log_g)
    pad = l_pad - l_real
    dmask = jnp.pad(dmask, ((0, 0), (0, pad), (0, pad)))
    q_dec = jnp.pad(q_dec.T, ((0, 0), (0, pad)))
    k_dec = jnp.pad(k_dec.T, ((0, 0), (0, pad)))
    qdec = jnp.broadcast_to(q_dec[:, :, None], (RET_HEADS, l_pad, vd))
    kdec = jnp.broadcast_to(k_dec[:, :, None], (RET_HEADS, l_pad, qk))
    cdec = jnp.broadcast_to(chunk_dec[:, None, None], (RET_HEADS, 1, vd))
    return dmask, qdec, kdec, cdec


def _rope_tables(pos0, t, qk):
    half = qk // 2
    inv = ROPE_BASE ** (-jnp.arange(half, dtype=F32) / half)
    pos = pos0 + jnp.arange(t)
    ang = pos.astype(F32)[:, None] * inv[None, :]
    return jnp.cos(ang), jnp.sin(ang)


def _relayout(x, n, t, src, dst):
    if src == dst:
        return x
    lead = (n, t) if src == 'bm' else (t, n)
    return x.reshape(lead + (-1,)).transpose(1, 0, 2).reshape(n * t, -1)


def _mixer_layout(layer):
    return 'bm' if layer % 3 == 1 else 'tm'


def _mlp_cast_jobs(wts, layer, steps):
    depth = wts['mlp_w_up'].shape[0]
    layers = [layer]
    while layers[-1] + 1 < depth and (layers[-1] + 1) % 3 == 2:
        layers.append(layers[-1] + 1)
    layers = [l for l in layers if l not in wts['mlp_bf16']]
    stacks = (wts['mlp_w_up'], wts['mlp_w_down'])
    if any(w.shape[1] % (steps * BF16_ROWS) for w in stacks):
        return [], []
    return layers, [(w, l) for l in layers for w in stacks]


def _mlp_weights(wts, layer):
    if layer not in wts['mlp_bf16']:
        wts['mlp_bf16'][layer] = (wts['mlp_w_up'][layer].astype(BF16), wts['mlp_w_down'][layer].astype(BF16))
    return wts['mlp_bf16'][layer]


def _trunk(x, pos0, st_re, st_im, st_ret, st_pool, wts):
    n, t, d = x.shape
    rows = n * t
    depth = wts['norm_mix'].shape[0]
    qk, vd = d // RET_HEADS, 2 * d // RET_HEADS
    tc = min(max(ROW_TILE // n, 1), t)
    in_kernel_relayout = _can_relayout_in_kernel(n, t)
    new_re, new_im, new_ret, new_pool = [], [], [], []

    xb, layout = x.reshape(rows, d), 'bm'
    for layer in range(depth):
        kind, j = layer % 3, layer // 3
        want = _mixer_layout(layer)
        if layout != want and not (kind == 0 and in_kernel_relayout):
            xb, layout = _relayout(xb, n, t, layout, want), want
        if kind == 1:
            l_real = math.gcd(t, RET_CHUNK)
            nc = t // l_real
            lb = max(l_real, BF16_ROWS)
            cos, sin = _rope_tables(pos0, t, qk)
            if t < ROW_TILE:
                cos, sin = jnp.tile(cos, (ROW_TILE // t, 1)), jnp.tile(sin, (ROW_TILE // t, 1))
            cast_layers, cast_jobs = _mlp_cast_jobs(wts, layer, rows // ROW_TILE)
            q, k, v, gate, *casts = _ret_in(xb, wts['norm_mix'], layer, wts['ret_w_in'], j, cos, sin, cast_jobs)
            if lb != l_real:
                pad = lambda z: jnp.pad(z.reshape(n * nc, l_real, -1), ((0, 0), (0, lb - l_real), (0, 0))
                                        ).reshape(n * nc * lb, -1)
                q, k, v = pad(q), pad(k), pad(v)
            consts = _ret_consts(l_real, max(l_real, RET_MIN_CHUNK_ROWS), qk, vd)
            o, s_new = _ret_core(q, k, v, None if st_ret is None else st_ret[j], consts, n, nc, lb)
            if lb != l_real:
                o = o.reshape(n * nc, lb, -1)[:, :l_real].reshape(rows, -1)
            proj, mixer = 'gated', (o, gate, wts['ret_w_out'], j)
            new_ret.append(s_new)
        elif kind == 0:
            cast_layers, cast_jobs = _mlp_cast_jobs(wts, layer, rows // (tc * n))
            gl, s_re, s_im, *casts = _s5_core(xb, layout, wts['norm_mix'], layer, wts['s5_packed'], wts['s5_d'], j,
                                              st_re[j].reshape(n, -1), st_im[j].reshape(n, -1), n, t, tc, cast_jobs)
            proj, mixer = 'glu', (gl, wts['s5_w_glu'], j)
            new_re.append(s_re.reshape(st_re[j].shape))
            new_im.append(s_im.reshape(st_im[j].shape))
        else:
            cast_layers, casts = [], []
            buf = st_pool[j]
            buf_t = buf.transpose(1, 0, 2).reshape(POOL_TAIL * n, d)
            if buf_t.size * buf_t.dtype.itemsize <= POOL_FUSE_MAX_TAIL_BYTES:
                proj, mixer = 'pool', (wts['norm_mix'], buf_t, wts['pool_w'], wts['pool_scale'], j)
            else:
                xb, tail = _pool(xb, wts['norm_mix'], layer, buf_t, wts['pool_w'], wts['pool_scale'], j, n, pos0)
                new_pool.append(tail.reshape(POOL_TAIL, n, d).transpose(1, 0, 2))
                proj, mixer = None, ()
        for idx, l in enumerate(cast_layers):
            wts['mlp_bf16'][l] = (casts[2 * idx], casts[2 * idx + 1])
        final = layer == depth - 1
        wanted = 'bm' if final else _mixer_layout(layer + 1)
        dst = wanted if in_kernel_relayout else layout
        w_up, w_down = _mlp_weights(wts, layer)
        xb, *tail = _mlp(proj, mixer, want, xb, layout, wts['norm_ffn'], w_up, w_down, layer,
                         wts['norm_final'], final, n, t, dst, pos0)
        if tail:
            new_pool.append(tail[0].reshape(POOL_TAIL, n, d).transpose(1, 0, 2))
        layout = dst
    y = _relayout(xb, n, t, layout, 'bm').reshape(n, t, d)
    return y, jnp.stack(new_re), jnp.stack(new_im), jnp.stack(new_ret), jnp.stack(new_pool)


def kernel(x_prompt, x_sample, state_s5_re, state_s5_im, state_ret, state_pool, norm_mix, norm_ffn, norm_final,
           s5_a_re, s5_a_im, s5_log_dt, s5_b_re, s5_b_im, s5_c_re, s5_c_im, s5_d, s5_w_glu, ret_w_in, ret_w_out,
           pool_w, pool_scale, mlp_w_up, mlp_w_down):
    nb = x_prompt.shape[0]
    depth, d = norm_mix.shape
    wts = dict(
        norm_mix=norm_mix.reshape(depth, 1, d), norm_ffn=norm_ffn.reshape(depth, 1, d), norm_final=norm_final,
        s5_d=s5_d.reshape(s5_d.shape[0], 1, d),
        s5_packed=jax.vmap(_s5_weights)(s5_a_re, s5_a_im, s5_log_dt, s5_b_re, s5_b_im, s5_c_re, s5_c_im),
        s5_w_glu=s5_w_glu.astype(BF16), ret_w_in=ret_w_in.astype(BF16), ret_w_out=ret_w_out.astype(BF16),
        pool_w=pool_w.astype(BF16), pool_scale=pool_scale.reshape(pool_scale.shape[0], 1, d),
        mlp_w_up=mlp_w_up, mlp_w_down=mlp_w_down, mlp_bf16={})
    z_re = jnp.zeros((state_s5_re.shape[0], nb) + state_s5_re.shape[2:], state_s5_re.dtype)
    z_im = jnp.zeros((state_s5_im.shape[0], nb) + state_s5_im.shape[2:], state_s5_im.dtype)
    z_pool = jnp.zeros((state_pool.shape[0], nb) + state_pool.shape[2:], state_pool.dtype)
    y_p, p_re, p_im, p_ret, p_pool = _trunk(x_prompt, 0, z_re, z_im, None, z_pool, wts)
    y_s, s_re, s_im, s_ret, s_pool = _trunk(x_sample, PAST_LEN, state_s5_re, state_s5_im, state_ret, state_pool, wts)
    return (y_p, y_s, p_re, p_im, p_ret, p_pool, s_re, s_im, s_ret, s_pool)
```

```python
import functools
import math

import jax
import jax.numpy as jnp
from jax import lax
from jax.experimental import pallas as pl
from jax.experimental.pallas import tpu as pltpu

F32 = jnp.float32
BF16 = jnp.bfloat16

EPS = 1e-6
RET_HEADS = 4
RET_CHUNK = 256
RET_MIN_CHUNK_ROWS = 128
ROPE_BASE = 10000.0
POOL_WINDOWS = (2, 4, 8, 16)
POOL_TAIL = max(POOL_WINDOWS) - 1

SUBLANES = 8
BF16_ROWS = 16
MXU_DIM = 256
ROW_TILE = 512
SCAN_LANES = 512
RET_SEQS_PER_STEP = 4
POOL_FUSE_MAX_TAIL_BYTES = 1 << 20
VMEM_LIMIT = 56 * 1024 * 1024
PAST_LEN = 16384


def _params(n_axes):
    return pltpu.CompilerParams(dimension_semantics=("arbitrary",) * n_axes, vmem_limit_bytes=VMEM_LIMIT)


def _const_spec(shape):
    zeros = (0,) * len(shape)
    return pl.BlockSpec(shape, lambda *_: zeros, pipeline_mode=pl.Buffered(1))


def _layer_spec(stacked_shape, layer):
    idx = (layer,) + (0,) * (len(stacked_shape) - 1)
    return pl.BlockSpec((None,) + tuple(stacked_shape[1:]), lambda *_: idx, pipeline_mode=pl.Buffered(1))


def _can_relayout_in_kernel(n, t):
    steps = ROW_TILE // n
    return ROW_TILE % n == 0 and steps % BF16_ROWS == 0 and t % steps == 0


def _tile_spec(layout, mixed, n, width):
    if mixed and layout == 'bm':
        return pl.BlockSpec((n, ROW_TILE // n, width), lambda i: (0, i, 0))
    return pl.BlockSpec((ROW_TILE, width), lambda i: (i, 0))


def _tile_view(x, layout, mixed, n, t):
    return x.reshape(n, t, x.shape[-1]) if mixed and layout == 'bm' else x


def _permute_rows(y, n, to):
    rows, d = y.shape
    lead = (rows // n, n) if to == 'bm' else (n, rows // n)
    return jnp.swapaxes(y.reshape(lead + (d,)), 0, 1).reshape(rows, d)


def _rms(x, g):
    ms = jnp.mean(x * x, axis=-1, keepdims=True)
    return x * lax.rsqrt(ms + EPS) * g


def _gated_heads_proj(o, gate, wp_ref):
    vd = o.shape[1] // RET_HEADS
    mix = None
    for hd in range(RET_HEADS):
        cols = slice(hd * vd, (hd + 1) * vd)
        oc = o[:, cols] - jnp.mean(o[:, cols], axis=-1, keepdims=True)
        on = oc * lax.rsqrt(jnp.mean(oc * oc, axis=-1, keepdims=True) + EPS)
        og = (on * jax.nn.silu(gate[:, cols].astype(F32))).astype(BF16)
        part = jnp.dot(og, wp_ref[cols, :], preferred_element_type=F32)
        mix = part if mix is None else mix + part
    return mix


def _pool_mix(x, g_ref, buf_ref, w_ref, sc_ref, tail_out, tail, *, n, pos0):
    i = pl.program_id(0)

    @pl.when(i == 0)
    def _():
        tail[...] = buf_ref[...]

    rows, d = x.shape
    tc = rows // n
    h = _rms(x, g_ref[...])
    ext = jnp.concatenate([tail[...], h], axis=0)
    grp = d // len(POOL_WINDOWS)
    t_idx = i * tc + lax.shift_right_logical(lax.broadcasted_iota(jnp.int32, (rows, grp), 0), n.bit_length() - 1)
    zs = []
    for gi, w in enumerate(POOL_WINDOWS):
        cols = slice(gi * grp, (gi + 1) * grp)
        s, first, span = ext[:, cols], 0, 1
        while span < w:
            s = s[span * n:] + s[:-span * n]
            first += span
            span *= 2
        s = s[(POOL_TAIL - first) * n:]
        cnt = jnp.minimum(w, pos0 + t_idx + 1).astype(F32)
        pooled = s / cnt - h[:, cols]
        zs.append(jnp.dot(pooled.astype(BF16), w_ref[gi], preferred_element_type=F32))
    new_tail = ext[rows:]
    tail[...] = new_tail

    @pl.when(i == pl.num_programs(0) - 1)
    def _():
        tail_out[...] = new_tail

    return jnp.concatenate(zs, axis=1) * sc_ref[...]


MIXER_REFS = {None: 0, 'glu': 2, 'gated': 3, 'pool': 4}


def _mlp_body(*refs, proj, final, ff_chunk, n, mix_to, out_to, pos0):
    mix_refs = refs[:MIXER_REFS[proj]]
    x_ref, g_ref, wu_ref, wd_ref, gf_ref, o_ref = refs[len(mix_refs):len(mix_refs) + 6]
    d = x_ref.shape[-1]
    x = x_ref[...].reshape(ROW_TILE, d)
    if proj == 'pool':
        x = x + _pool_mix(x, *mix_refs, *refs[len(mix_refs) + 6:], n=n, pos0=pos0)
    elif proj is not None:
        a = mix_refs[0][...].reshape(ROW_TILE, mix_refs[0].shape[-1])
        if proj == 'glu':
            mix = jnp.dot(a, mix_refs[1][...], preferred_element_type=F32)
            mix = mix[:, :d] * jax.nn.sigmoid(mix[:, d:])
        else:
            mix = _gated_heads_proj(a, mix_refs[1][...].reshape(a.shape), mix_refs[2])
        if mix_to is not None:
            mix = _permute_rows(mix, n, mix_to)
        x = x + mix
    h = _rms(x, g_ref[...]).astype(BF16)
    acc = jnp.zeros_like(x)
    d_ff = wu_ref.shape[1]
    for c in range(d_ff // ff_chunk):
        cols = slice(c * ff_chunk, (c + 1) * ff_chunk)
        a = jnp.dot(h, wu_ref[:, cols], preferred_element_type=F32)
        a = jnp.square(jnp.maximum(a, 0.0)).astype(BF16)
        acc = acc + jnp.dot(a, wd_ref[cols, :], preferred_element_type=F32)
    y = x + acc
    if final:
        y = _rms(y, gf_ref[...])
    if out_to is not None:
        y = _permute_rows(y, n, out_to)
    o_ref[...] = y.reshape(o_ref.shape)


def _mlp(proj, mixer, a_layout, x, x_layout, g, w_up, w_down, layer, g_final, final, n, t, dst, pos0):
    d = x.shape[1]
    j = mixer[-1] if mixer else None
    mixed = len({x_layout, dst} | ({a_layout} if proj in ('glu', 'gated') else set())) > 1
    assert not mixed or _can_relayout_in_kernel(n, t)
    out_shape = [jax.ShapeDtypeStruct((n, t, d) if mixed and dst == 'bm' else (n * t, d), F32)]
    out_specs = [_tile_spec(dst, mixed, n, d)]
    scratch = []
    if proj == 'pool':
        assert x_layout == 'tm' and ROW_TILE % n == 0 and n & (n - 1) == 0
        g_mix, tail0, w_pool, scale = mixer[:-1]
        operands = [g_mix, tail0, w_pool, scale]
        specs = [_layer_spec(g_mix.shape, layer), _const_spec(tail0.shape), _layer_spec(w_pool.shape, j),
                 _layer_spec(scale.shape, j)]
        out_shape.append(jax.ShapeDtypeStruct(tail0.shape, F32))
        out_specs.append(pl.BlockSpec(tail0.shape, lambda i: (0, 0)))
        scratch.append(pltpu.VMEM(tail0.shape, F32))
    elif proj is None:
        operands, specs = [], []
    else:
        rows_in, w_proj = mixer[:-2], mixer[-2]
        operands = [_tile_view(z, a_layout, mixed, n, t) for z in rows_in] + [w_proj]
        specs = [_tile_spec(a_layout, mixed, n, z.shape[1]) for z in rows_in] + [_const_spec(w_proj.shape)]
    operands += [_tile_view(x, x_layout, mixed, n, t), g, w_up, w_down, g_final.reshape(1, d)]
    specs += [_tile_spec(x_layout, mixed, n, d), _layer_spec(g.shape, layer), _const_spec(w_up.shape),
              _const_spec(w_down.shape), _const_spec((1, d))]
    outs = pl.pallas_call(
        functools.partial(_mlp_body, proj=proj, final=final, ff_chunk=1024, n=n, pos0=pos0,
                          mix_to=x_layout if proj in ('glu', 'gated') and a_layout != x_layout else None,
                          out_to=dst if dst != x_layout else None),
        grid=(n * t // ROW_TILE,),
        in_specs=specs,
        out_specs=out_specs,
        out_shape=out_shape,
        scratch_shapes=scratch,
        compiler_params=_params(1),
        name=(proj + "_mlp" if proj else "mlp") + ("_final" if final else ""),
    )(*operands)
    return (outs[0].reshape(n * t, d),) + tuple(outs[1:])


def _pool_body(g_ref, buf_ref, w_ref, sc_ref, x_ref, o_ref, tail_out, tail, *, n, pos0):
    x = x_ref[...]
    o_ref[...] = x + _pool_mix(x, g_ref, buf_ref, w_ref, sc_ref, tail_out, tail, n=n, pos0=pos0)


def _pool(x, g_mix, layer, tail0, w_pool, scale, j, n, pos0):
    rows, d = x.shape
    assert ROW_TILE % n == 0 and n & (n - 1) == 0
    return pl.pallas_call(
        functools.partial(_pool_body, n=n, pos0=pos0),
        grid=(rows // ROW_TILE,),
        in_specs=[_layer_spec(g_mix.shape, layer), _const_spec(tail0.shape), _layer_spec(w_pool.shape, j),
                  _layer_spec(scale.shape, j), pl.BlockSpec((ROW_TILE, d), lambda i: (i, 0))],
        out_specs=[pl.BlockSpec((ROW_TILE, d), lambda i: (i, 0)), pl.BlockSpec(tail0.shape, lambda i: (0, 0))],
        out_shape=[jax.ShapeDtypeStruct((rows, d), F32), jax.ShapeDtypeStruct(tail0.shape, F32)],
        scratch_shapes=[pltpu.VMEM(tail0.shape, F32)],
        compiler_params=_params(1),
        name="pool_mixer",
    )(g_mix, tail0, w_pool, scale, x)


def _cast_specs(jobs, steps):
    operands, in_specs, out_specs, out_shapes = [], [], [], []
    for w, layer in jobs:
        _, rows, cols = w.shape
        assert rows % (steps * BF16_ROWS) == 0
        blk = rows // steps
        operands.append(w)
        in_specs.append(pl.BlockSpec((None, blk, cols), lambda i, layer=layer: (layer, i, 0)))
        out_specs.append(pl.BlockSpec((blk, cols), lambda i: (i, 0)))
        out_shapes.append(jax.ShapeDtypeStruct((rows, cols), BF16))
    return operands, in_specs, out_specs, out_shapes


def _run_casts(src_refs, dst_refs):
    for src, dst in zip(src_refs, dst_refs, strict=True):
        dst[...] = src[...].astype(BF16)


def _s5_body(*refs, nb, tc, n_cast):
    x_ref, g_ref, bw_ref, are_ref, aim_ref, cre_ref, cim_ref, dsk_ref, s0re_ref, s0im_ref = refs[:10]
    cast_src, refs = refs[10:10 + n_cast], refs[10 + n_cast:]
    gl_ref, sre_out, sim_out = refs[:3]
    cast_dst, (bu_re, bu_im, st_re, st_im) = refs[3:3 + n_cast], refs[3 + n_cast:]
    _run_casts(cast_src, cast_dst)
    i = pl.program_id(0)

    @pl.when(i == 0)
    def _():
        st_re[...] = s0re_ref[...]
        st_im[...] = s0im_ref[...]

    d = x_ref.shape[-1]
    x = x_ref[...].reshape(tc * nb, d)
    if len(x_ref.shape) == 3:
        x = _permute_rows(x, nb, 'tm')
    h = _rms(x, g_ref[...])
    hb = h.astype(BF16)
    n_state_slab = bw_ref.shape[2] // 2
    ys = []
    for j in range(bw_ref.shape[0]):
        r = jnp.dot(hb[:, j * MXU_DIM:(j + 1) * MXU_DIM], bw_ref[j], preferred_element_type=F32)
        cols = slice(j * n_state_slab, (j + 1) * n_state_slab)
        bu_re[:, cols] = r[:, :n_state_slab]
        bu_im[:, cols] = r[:, n_state_slab:]
        for c in range(n_state_slab // SCAN_LANES):
            lanes = slice(j * n_state_slab + c * SCAN_LANES, j * n_state_slab + (c + 1) * SCAN_LANES)
            ar = jnp.broadcast_to(are_ref[:, lanes], (SUBLANES, SCAN_LANES))
            ai = jnp.broadcast_to(aim_ref[:, lanes], (SUBLANES, SCAN_LANES))
            for rg in range(nb // SUBLANES):
                seqs = slice(rg * SUBLANES, (rg + 1) * SUBLANES)
                sr, si = st_re[seqs, lanes], st_im[seqs, lanes]
                for t in range(tc):
                    rows = slice(t * nb + rg * SUBLANES, t * nb + (rg + 1) * SUBLANES)
                    sr, si = (ar * sr - ai * si + bu_re[rows, lanes], ar * si + ai * sr + bu_im[rows, lanes])
                    bu_re[rows, lanes] = sr
                    bu_im[rows, lanes] = si
                st_re[seqs, lanes] = sr
                st_im[seqs, lanes] = si
        y = jnp.dot(bu_re[:, cols].astype(BF16), cre_ref[j], preferred_element_type=F32)
        ys.append(y + jnp.dot(bu_im[:, cols].astype(BF16), cim_ref[j], preferred_element_type=F32))
    y = jnp.concatenate(ys, axis=1) + dsk_ref[...] * h
    gl_ref[...] = jax.nn.gelu(y, approximate=True).astype(BF16)

    @pl.when(i == pl.num_programs(0) - 1)
    def _():
        sre_out[...] = st_re[...]
        sim_out[...] = st_im[...]


def _s5_core(x, x_layout, g, layer, packed, d_skip, j, s0_re, s0_im, nb, t, tc, cast_jobs):
    rows, d = x.shape
    bw, a_re, a_im, c_re, c_im = packed
    n_lanes = a_re.shape[-1]
    r = tc * nb
    from_bm = x_layout == 'bm'
    assert not from_bm or (r == ROW_TILE and _can_relayout_in_kernel(nb, t))
    cast_ops, cast_in, cast_out, cast_shapes = _cast_specs(cast_jobs, rows // r)
    return pl.pallas_call(
        functools.partial(_s5_body, nb=nb, tc=tc, n_cast=len(cast_jobs)),
        grid=(rows // r,),
        in_specs=[_tile_spec(x_layout, from_bm, nb, d) if from_bm else pl.BlockSpec((r, d), lambda i: (i, 0)),
                  _layer_spec(g.shape, layer), _layer_spec(bw.shape, j), _layer_spec(a_re.shape, j),
                  _layer_spec(a_im.shape, j), _layer_spec(c_re.shape, j), _layer_spec(c_im.shape, j),
                  _layer_spec(d_skip.shape, j),
                  _const_spec((nb, n_lanes)), _const_spec((nb, n_lanes))] + cast_in,
        out_specs=[pl.BlockSpec((r, d), lambda i: (i, 0)),
                   pl.BlockSpec((nb, n_lanes), lambda i: (0, 0)), pl.BlockSpec((nb, n_lanes), lambda i: (0, 0))]
                  + cast_out,
        out_shape=[jax.ShapeDtypeStruct((rows, d), BF16),
                   jax.ShapeDtypeStruct((nb, n_lanes), F32), jax.ShapeDtypeStruct((nb, n_lanes), F32)] + cast_shapes,
        scratch_shapes=[pltpu.VMEM((r, n_lanes), F32), pltpu.VMEM((r, n_lanes), F32),
                        pltpu.VMEM((nb, n_lanes), F32), pltpu.VMEM((nb, n_lanes), F32)],
        compiler_params=_params(1),
        name="s5_core",
    )(_tile_view(x, x_layout, from_bm, nb, t), g, bw, a_re, a_im, c_re, c_im, d_skip, s0_re, s0_im, *cast_ops)


def _s5_weights(a_re, a_im, log_dt, b_re, b_im, c_re, c_im):
    groups, p, hch = b_re.shape
    dt = jnp.exp(log_dt.astype(F32))[:, None]
    lr, li = a_re.astype(F32), a_im.astype(F32)
    mag = jnp.exp(lr * dt)
    ab_re, ab_im = mag * jnp.cos(li * dt), mag * jnp.sin(li * dt)
    den = lr * lr + li * li
    nr, ni = ab_re - 1.0, ab_im
    f_re = (nr * lr + ni * li) / den
    f_im = (ni * lr - nr * li) / den
    br, bi = b_re.astype(F32), b_im.astype(F32)
    bb_re = f_re[..., None] * br - f_im[..., None] * bi
    bb_im = f_re[..., None] * bi + f_im[..., None] * br
    gs = MXU_DIM // hch
    n_slab = groups // gs

    def block_diag(per_group):
        r, c = per_group.shape[1:]
        tiled = jnp.tile(per_group.reshape(n_slab, gs * r, c), (1, 1, gs))
        row_g = lax.broadcasted_iota(jnp.int32, (gs * r, gs * c), 0) // r
        col_g = lax.broadcasted_iota(jnp.int32, (gs * r, gs * c), 1) // c
        return jnp.where(row_g == col_g, tiled, 0.0).astype(BF16)

    to_hp = lambda bb: bb.transpose(0, 2, 1)
    bw = jnp.concatenate([block_diag(to_hp(bb_re)), block_diag(to_hp(bb_im))], axis=2)
    cw_re = block_diag(c_re.astype(F32).transpose(0, 2, 1))
    cw_im = block_diag(-c_im.astype(F32).transpose(0, 2, 1))
    return bw, ab_re.reshape(1, groups * p), ab_im.reshape(1, groups * p), cw_re, cw_im


def _ret_in_body(*refs, d, qk, n_cast):
    x_ref, g_ref, w_ref, cos_ref, sin_ref = refs[:5]
    q_ref, k_ref, v_ref, gate_ref = refs[5 + n_cast:9 + n_cast]
    _run_casts(refs[5:5 + n_cast], refs[9 + n_cast:])
    h = _rms(x_ref[...], g_ref[...]).astype(BF16)
    cos, sin = cos_ref[...], sin_ref[...]
    half = qk // 2

    def rot(z):
        outs = []
        for hd in range(d // qk):
            z1 = z[:, hd * qk:hd * qk + half]
            z2 = z[:, hd * qk + half:(hd + 1) * qk]
            outs += [z1 * cos - z2 * sin, z1 * sin + z2 * cos]
        return jnp.concatenate(outs, axis=1)

    q = jnp.dot(h, w_ref[:, 0:d], preferred_element_type=F32)
    q_ref[...] = (rot(q) * (qk ** -0.5)).astype(BF16)
    k = jnp.dot(h, w_ref[:, d:2 * d], preferred_element_type=F32)
    k_ref[...] = rot(k).astype(BF16)
    v_ref[...] = jnp.dot(h, w_ref[:, 2 * d:4 * d], preferred_element_type=F32).astype(BF16)
    gate_ref[...] = jnp.dot(h, w_ref[:, 4 * d:6 * d], preferred_element_type=F32).astype(BF16)


def _ret_in(x, g, layer, w_in, cos_tab, sin_tab, cast_jobs):
    rows, d = x.shape
    qk = d // RET_HEADS
    tab_blocks = cos_tab.shape[0] // ROW_TILE
    row_spec = lambda width: pl.BlockSpec((ROW_TILE, width), lambda i: (i, 0))
    tab_spec = pl.BlockSpec((ROW_TILE, qk // 2), lambda i: (i % tab_blocks, 0))
    cast_ops, cast_in, cast_out, cast_shapes = _cast_specs(cast_jobs, rows // ROW_TILE)
    return pl.pallas_call(
        functools.partial(_ret_in_body, d=d, qk=qk, n_cast=len(cast_jobs)),
        grid=(rows // ROW_TILE,),
        in_specs=[row_spec(d), _layer_spec(g.shape, layer), _const_spec(w_in.shape), tab_spec, tab_spec] + cast_in,
        out_specs=[row_spec(d), row_spec(d), row_spec(2 * d), row_spec(2 * d)] + cast_out,
        out_shape=[jax.ShapeDtypeStruct((rows, d), BF16), jax.ShapeDtypeStruct((rows, d), BF16),
                   jax.ShapeDtypeStruct((rows, 2 * d), BF16), jax.ShapeDtypeStruct((rows, 2 * d), BF16)] + cast_shapes,
        compiler_params=_params(1),
        name="ret_in",
    )(x, g, w_in, cos_tab, sin_tab, *cast_ops)


def _ret_core_body(q_ref, k_ref, v_ref, dmask_ref, qdec_ref, kdec_ref, cdec_ref, *rest,
                   lb, lp, qk, vd, seqs, chunks):
    o_ref, s_ref = rest[-2:]
    c = pl.program_id(1)

    @pl.when(c == 0)
    def _():
        s_ref[...] = rest[0][...] if len(rest) == 3 else jnp.zeros(s_ref.shape, s_ref.dtype)

    def padded(z):
        if lb == lp:
            return z
        return jnp.concatenate([z, jnp.zeros((lp - lb, z.shape[1]), z.dtype)], axis=0)

    for blk in range(seqs * chunks):
        b = blk // chunks
        rows = slice(blk * lb, (blk + 1) * lb)
        q, k, v = padded(q_ref[rows, :]), padded(k_ref[rows, :]), padded(v_ref[rows, :])
        for hd in range(RET_HEADS):
            qh = q[:, hd * qk:(hd + 1) * qk]
            kh = k[:, hd * qk:(hd + 1) * qk]
            kf = kh.astype(F32)
            vh = v[:, hd * vd:(hd + 1) * vd]
            s_old = s_ref[b, hd]
            scores = lax.dot_general(qh, kh, (((1,), (1,)), ((), ())), preferred_element_type=F32) * dmask_ref[hd]
            o = jnp.dot(scores.astype(BF16), vh, preferred_element_type=F32)
            o = o + qdec_ref[hd] * jnp.dot(qh, s_old.astype(BF16), preferred_element_type=F32)
            kd = (kf * kdec_ref[hd]).astype(BF16)
            s_ref[b, hd] = cdec_ref[hd] * s_old + lax.dot_general(kd, vh, (((0,), (0,)), ((), ())),
                                                                  preferred_element_type=F32)
            o_ref[rows, hd * vd:(hd + 1) * vd] = o[:lb]


def _ret_core(q, k, v, s0, consts, n, nc, lb):
    rows, d = q.shape
    qk, vd = d // RET_HEADS, 2 * d // RET_HEADS
    dmask, qdec, kdec, cdec = consts
    lp = dmask.shape[1]
    chunks = math.gcd(nc, max(ROW_TILE // lb, 1))
    seqs = math.gcd(n, RET_SEQS_PER_STEP) if chunks == nc else 1
    nc_blocks = nc // chunks
    row_spec = lambda width: pl.BlockSpec((seqs * chunks * lb, width), lambda b, c: (b * nc_blocks + c, 0))
    s_spec = pl.BlockSpec((seqs, RET_HEADS, qk, vd), lambda b, c: (b, 0, 0, 0))
    return pl.pallas_call(
        functools.partial(_ret_core_body, lb=lb, lp=lp, qk=qk, vd=vd, seqs=seqs, chunks=chunks),
        grid=(n // seqs, nc_blocks),
        in_specs=[row_spec(d), row_spec(d), row_spec(2 * d),
                  _const_spec(dmask.shape), _const_spec(qdec.shape), _const_spec(kdec.shape), _const_spec(cdec.shape)]
                 + ([] if s0 is None else [s_spec]),
        out_specs=[row_spec(2 * d), s_spec],
        out_shape=[jax.ShapeDtypeStruct((rows, 2 * d), F32), jax.ShapeDtypeStruct((n, RET_HEADS, qk, vd), F32)],
        compiler_params=_params(2),
        name="ret_core",
    )(q, k, v, dmask, qdec, kdec, cdec, *(() if s0 is None else (s0,)))


def _ret_consts(l_real, l_pad, qk, vd):
    log_g = jnp.log1p(-jnp.exp2(-5.0 - jnp.arange(RET_HEADS, dtype=F32)))
    idx = jnp.arange(l_real, dtype=F32)
    diff = idx[:, None] - idx[None, :]
    dmask = jnp.exp(jnp.maximum(diff, 0.0)[None] * log_g[:, None, None]) * (diff >= 0)[None]
    q_dec = jnp.exp((idx + 1.0)[:, None] * log_g[None, :])
    k_dec = jnp.exp((l_real - 1.0 - idx)[:, None] * log_g[None, :])
    chunk_dec = jnp.exp(l_real * log_g)
    pad = l_pad - l_real
    dmask = jnp.pad(dmask, ((0, 0), (0, pad), (0, pad)))
    q_dec = jnp.pad(q_dec.T, ((0, 0), (0, pad)))
    k_dec = jnp.pad(k_dec.T, ((0, 0), (0, pad)))
    qdec = jnp.broadcast_to(q_dec[:, :, None], (RET_HEADS, l_pad, vd))
    kdec = jnp.broadcast_to(k_dec[:, :, None], (RET_HEADS, l_pad, qk))
    cdec = jnp.broadcast_to(chunk_dec[:, None, None], (RET_HEADS, 1, vd))
    return dmask, qdec, kdec, cdec


def _rope_tables(pos0, t, qk):
    half = qk // 2
    inv = ROPE_BASE ** (-jnp.arange(half, dtype=F32) / half)
    pos = pos0 + jnp.arange(t)
    ang = pos.astype(F32)[:, None] * inv[None, :]
    return jnp.cos(ang), jnp.sin(ang)


def _relayout(x, n, t, src, dst):
    if src == dst:
        return x
    lead = (n, t) if src == 'bm' else (t, n)
    return x.reshape(lead + (-1,)).transpose(1, 0, 2).reshape(n * t, -1)


def _mixer_layout(layer):
    return 'bm' if layer % 3 == 1 else 'tm'


def _cast_jobs(wts, layer, steps):
    depth = wts['mlp_w_up'].shape[0]
    mlp_layers = [layer]
    while mlp_layers[-1] + 1 < depth and (mlp_layers[-1] + 1) % 3 == 2:
        mlp_layers.append(mlp_layers[-1] + 1)
    keys = [(name, l) for l in mlp_layers for name in ('mlp_w_up', 'mlp_w_down')]
    if layer % 3 == 0:
        keys.append(('s5_w_glu', layer // 3))
        if layer + 1 < depth and (layer + 1) % 3 == 1:
            keys.append(('ret_w_in', (layer + 1) // 3))
    elif layer % 3 == 1:
        keys.append(('ret_w_out', layer // 3))
    keys = [k for k in keys if k not in wts['bf16'] and wts[k[0]].shape[1] % (steps * BF16_ROWS) == 0]
    return keys, [(wts[name], idx) for name, idx in keys]


def _bf16_weight(wts, name, idx):
    if (name, idx) not in wts['bf16']:
        wts['bf16'][(name, idx)] = wts[name][idx].astype(BF16)
    return wts['bf16'][(name, idx)]


def _trunk(x, pos0, st_re, st_im, st_ret, st_pool, wts):
    n, t, d = x.shape
    rows = n * t
    depth = wts['norm_mix'].shape[0]
    qk, vd = d // RET_HEADS, 2 * d // RET_HEADS
    tc = min(max(ROW_TILE // n, 1), t)
    in_kernel_relayout = _can_relayout_in_kernel(n, t)
    new_re, new_im, new_ret, new_pool = [], [], [], []

    xb, layout = x.reshape(rows, d), 'bm'
    for layer in range(depth):
        kind, j = layer % 3, layer // 3
        want = _mixer_layout(layer)
        if layout != want and not (kind == 0 and in_kernel_relayout):
            xb, layout = _relayout(xb, n, t, layout, want), want
        if kind == 1:
            l_real = math.gcd(t, RET_CHUNK)
            nc = t // l_real
            lb = max(l_real, BF16_ROWS)
            cos, sin = _rope_tables(pos0, t, qk)
            if t < ROW_TILE:
                cos, sin = jnp.tile(cos, (ROW_TILE // t, 1)), jnp.tile(sin, (ROW_TILE // t, 1))
            cast_keys, cast_jobs = _cast_jobs(wts, layer, rows // ROW_TILE)
            q, k, v, gate, *casts = _ret_in(xb, wts['norm_mix'], layer, _bf16_weight(wts, 'ret_w_in', j), cos, sin,
                                            cast_jobs)
            if lb != l_real:
                pad = lambda z: jnp.pad(z.reshape(n * nc, l_real, -1), ((0, 0), (0, lb - l_real), (0, 0))
                                        ).reshape(n * nc * lb, -1)
                q, k, v = pad(q), pad(k), pad(v)
            consts = _ret_consts(l_real, max(l_real, RET_MIN_CHUNK_ROWS), qk, vd)
            o, s_new = _ret_core(q, k, v, None if st_ret is None else st_ret[j], consts, n, nc, lb)
            if lb != l_real:
                o = o.reshape(n * nc, lb, -1)[:, :l_real].reshape(rows, -1)
            proj, mixer = 'gated', (o, gate, ('ret_w_out', j), j)
            new_ret.append(s_new)
        elif kind == 0:
            cast_keys, cast_jobs = _cast_jobs(wts, layer, rows // (tc * n))
            gl, s_re, s_im, *casts = _s5_core(xb, layout, wts['norm_mix'], layer, wts['s5_packed'], wts['s5_d'], j,
                                              st_re[j].reshape(n, -1), st_im[j].reshape(n, -1), n, t, tc, cast_jobs)
            proj, mixer = 'glu', (gl, ('s5_w_glu', j), j)
            new_re.append(s_re.reshape(st_re[j].shape))
            new_im.append(s_im.reshape(st_im[j].shape))
        else:
            cast_keys, casts = [], []
            buf = st_pool[j]
            buf_t = buf.transpose(1, 0, 2).reshape(POOL_TAIL * n, d)
            if buf_t.size * buf_t.dtype.itemsize <= POOL_FUSE_MAX_TAIL_BYTES:
                proj, mixer = 'pool', (wts['norm_mix'], buf_t, wts['pool_w'], wts['pool_scale'], j)
            else:
                xb, tail = _pool(xb, wts['norm_mix'], layer, buf_t, wts['pool_w'], wts['pool_scale'], j, n, pos0)
                new_pool.append(tail.reshape(POOL_TAIL, n, d).transpose(1, 0, 2))
                proj, mixer = None, ()
        wts['bf16'].update(zip(cast_keys, casts, strict=True))
        if proj in ('glu', 'gated'):
            mixer = mixer[:-2] + (_bf16_weight(wts, *mixer[-2]), j)
        final = layer == depth - 1
        wanted = 'bm' if final else _mixer_layout(layer + 1)
        dst = wanted if in_kernel_relayout else layout
        xb, *tail = _mlp(proj, mixer, want, xb, layout, wts['norm_ffn'], _bf16_weight(wts, 'mlp_w_up', layer),
                         _bf16_weight(wts, 'mlp_w_down', layer), layer, wts['norm_final'], final, n, t, dst, pos0)
        if tail:
            new_pool.append(tail[0].reshape(POOL_TAIL, n, d).transpose(1, 0, 2))
        layout = dst
    y = _relayout(xb, n, t, layout, 'bm').reshape(n, t, d)
    return y, jnp.stack(new_re), jnp.stack(new_im), jnp.stack(new_ret), jnp.stack(new_pool)


def kernel(x_prompt, x_sample, state_s5_re, state_s5_im, state_ret, state_pool, norm_mix, norm_ffn, norm_final,
           s5_a_re, s5_a_im, s5_log_dt, s5_b_re, s5_b_im, s5_c_re, s5_c_im, s5_d, s5_w_glu, ret_w_in, ret_w_out,
           pool_w, pool_scale, mlp_w_up, mlp_w_down):
    nb = x_prompt.shape[0]
    depth, d = norm_mix.shape
    wts = dict(
        norm_mix=norm_mix.reshape(depth, 1, d), norm_ffn=norm_ffn.reshape(depth, 1, d), norm_final=norm_final,
        s5_d=s5_d.reshape(s5_d.shape[0], 1, d),
        s5_packed=jax.vmap(_s5_weights)(s5_a_re, s5_a_im, s5_log_dt, s5_b_re, s5_b_im, s5_c_re, s5_c_im),
        s5_w_glu=s5_w_glu, ret_w_in=ret_w_in, ret_w_out=ret_w_out, mlp_w_up=mlp_w_up, mlp_w_down=mlp_w_down,
        bf16={},
        pool_w=pool_w.astype(BF16), pool_scale=pool_scale.reshape(pool_scale.shape[0], 1, d))
    z_re = jnp.zeros((state_s5_re.shape[0], nb) + state_s5_re.shape[2:], state_s5_re.dtype)
    z_im = jnp.zeros((state_s5_im.shape[0], nb) + state_s5_im.shape[2:], state_s5_im.dtype)
    z_pool = jnp.zeros((state_pool.shape[0], nb) + state_pool.shape[2:], state_pool.dtype)
    y_p, p_re, p_im, p_ret, p_pool = _trunk(x_prompt, 0, z_re, z_im, None, z_pool, wts)
    y_s, s_re, s_im, s_ret, s_pool = _trunk(x_sample, PAST_LEN, state_s5_re, state_s5_im, state_ret, state_pool, wts)
    return (y_p, y_s, p_re, p_im, p_ret, p_pool, s_re, s_im, s_ret, s_pool)
```

```python
import functools
import math

import jax
import jax.numpy as jnp
from jax import lax
from jax.experimental import pallas as pl
from jax.experimental.pallas import tpu as pltpu

F32 = jnp.float32
BF16 = jnp.bfloat16

EPS = 1e-6
RET_HEADS = 4
RET_CHUNK = 256
RET_MIN_CHUNK_ROWS = 128
ROPE_BASE = 10000.0
POOL_WINDOWS = (2, 4, 8, 16)
POOL_TAIL = max(POOL_WINDOWS) - 1

SUBLANES = 8
BF16_ROWS = 16
MXU_DIM = 256
ROW_TILE = 512
MLP_FF_CHUNK = 1024
SCAN_LANES = 512
RET_SEQS_PER_STEP = 4
POOL_FUSE_MAX_TAIL_BYTES = 1 << 20
VMEM_LIMIT = 56 * 1024 * 1024
PAST_LEN = 16384


def _params(n_axes):
    return pltpu.CompilerParams(dimension_semantics=("arbitrary",) * n_axes, vmem_limit_bytes=VMEM_LIMIT)


def _const_spec(shape):
    zeros = (0,) * len(shape)
    return pl.BlockSpec(shape, lambda *_: zeros, pipeline_mode=pl.Buffered(1))


def _layer_spec(stacked_shape, layer):
    idx = (layer,) + (0,) * (len(stacked_shape) - 1)
    return pl.BlockSpec((None,) + tuple(stacked_shape[1:]), lambda *_: idx, pipeline_mode=pl.Buffered(1))


def _can_relayout_in_kernel(n, t):
    steps = ROW_TILE // n
    return ROW_TILE % n == 0 and steps % BF16_ROWS == 0 and t % steps == 0


def _tile_spec(layout, mixed, n, width):
    if mixed and layout == 'bm':
        return pl.BlockSpec((n, ROW_TILE // n, width), lambda i: (0, i, 0))
    return pl.BlockSpec((ROW_TILE, width), lambda i: (i, 0))


def _tile_view(x, layout, mixed, n, t):
    return x.reshape(n, t, x.shape[-1]) if mixed and layout == 'bm' else x


def _permute_rows(y, n, to):
    rows, d = y.shape
    lead = (rows // n, n) if to == 'bm' else (n, rows // n)
    return jnp.swapaxes(y.reshape(lead + (d,)), 0, 1).reshape(rows, d)


def _rms(x, g):
    ms = jnp.mean(x * x, axis=-1, keepdims=True)
    return x * lax.rsqrt(ms + EPS) * g


def _gated_heads_proj(o, gate, wp_ref):
    vd = o.shape[1] // RET_HEADS
    mix = None
    for hd in range(RET_HEADS):
        cols = slice(hd * vd, (hd + 1) * vd)
        oc = o[:, cols] - jnp.mean(o[:, cols], axis=-1, keepdims=True)
        on = oc * lax.rsqrt(jnp.mean(oc * oc, axis=-1, keepdims=True) + EPS)
        og = (on * jax.nn.silu(gate[:, cols].astype(F32))).astype(BF16)
        part = jnp.dot(og, wp_ref[cols, :], preferred_element_type=F32)
        mix = part if mix is None else mix + part
    return mix


def _pool_mix(x, g_ref, buf_ref, w_ref, sc_ref, tail_out, tail, *, n, pos0):
    i = pl.program_id(0)

    @pl.when(i == 0)
    def _():
        tail[...] = buf_ref[...]

    rows, d = x.shape
    tc = rows // n
    h = _rms(x, g_ref[...])
    ext = jnp.concatenate([tail[...], h], axis=0)
    grp = d // len(POOL_WINDOWS)
    t_idx = i * tc + lax.shift_right_logical(lax.broadcasted_iota(jnp.int32, (rows, grp), 0), n.bit_length() - 1)
    zs = []
    for gi, w in enumerate(POOL_WINDOWS):
        cols = slice(gi * grp, (gi + 1) * grp)
        s, first, span = ext[:, cols], 0, 1
        while span < w:
            s = s[span * n:] + s[:-span * n]
            first += span
            span *= 2
        s = s[(POOL_TAIL - first) * n:]
        cnt = jnp.minimum(w, pos0 + t_idx + 1).astype(F32)
        pooled = s / cnt - h[:, cols]
        zs.append(jnp.dot(pooled.astype(BF16), w_ref[gi], preferred_element_type=F32))
    new_tail = ext[rows:]
    tail[...] = new_tail

    @pl.when(i == pl.num_programs(0) - 1)
    def _():
        tail_out[...] = new_tail

    return jnp.concatenate(zs, axis=1) * sc_ref[...]


MIXER_REFS = {None: 0, 'glu': 2, 'gated': 3, 'pool': 4}


def _mlp_body(*refs, proj, final, ff_chunk, n, mix_to, out_to, pos0, stream):
    mix_refs = refs[:MIXER_REFS[proj]]
    x_ref, g_ref, wu_ref, wd_ref, gf_ref, o_ref = refs[len(mix_refs):len(mix_refs) + 6]
    d = x_ref.shape[-1]

    def residual():
        x = x_ref[...].reshape(ROW_TILE, d)
        if proj == 'pool':
            return x + _pool_mix(x, *mix_refs, *refs[len(mix_refs) + 6:], n=n, pos0=pos0)
        if proj is None:
            return x
        a = mix_refs[0][...].reshape(ROW_TILE, mix_refs[0].shape[-1])
        if proj == 'glu':
            mix = jnp.dot(a, mix_refs[1][...], preferred_element_type=F32)
            mix = mix[:, :d] * jax.nn.sigmoid(mix[:, d:])
        else:
            mix = _gated_heads_proj(a, mix_refs[1][...].reshape(a.shape), mix_refs[2])
        if mix_to is not None:
            mix = _permute_rows(mix, n, mix_to)
        return x + mix

    def hidden_chunk(h, wu, wd):
        a = jnp.dot(h, wu, preferred_element_type=F32)
        return jnp.dot(jnp.square(jnp.maximum(a, 0.0)).astype(BF16), wd, preferred_element_type=F32)

    def finish(x, acc):
        y = x + acc
        if final:
            y = _rms(y, gf_ref[...])
        if out_to is not None:
            y = _permute_rows(y, n, out_to)
        o_ref[...] = y.reshape(o_ref.shape)

    if not stream:
        x = residual()
        h = _rms(x, g_ref[...]).astype(BF16)
        acc = jnp.zeros_like(x)
        for c in range(wu_ref.shape[1] // ff_chunk):
            cols = slice(c * ff_chunk, (c + 1) * ff_chunk)
            acc = acc + hidden_chunk(h, wu_ref[:, cols], wd_ref[cols, :])
        finish(x, acc)
        return

    x1, hb, acc = refs[-3:]
    c = pl.program_id(0)

    @pl.when(c == 0)
    def _():
        x = residual()
        x1[...] = x
        hb[...] = _rms(x, g_ref[...]).astype(BF16)
        acc[...] = jnp.zeros_like(x)

    acc[...] += hidden_chunk(hb[...], wu_ref[...], wd_ref[...])

    @pl.when(c == pl.num_programs(0) - 1)
    def _():
        finish(x1[...], acc[...])


def _mlp(proj, mixer, a_layout, x, x_layout, g, w_up, w_down, layer, g_final, final, n, t, dst, pos0):
    d = x.shape[1]
    j = mixer[-1] if mixer else None
    mixed = len({x_layout, dst} | ({a_layout} if proj in ('glu', 'gated') else set())) > 1
    assert not mixed or _can_relayout_in_kernel(n, t)
    out_shape = [jax.ShapeDtypeStruct((n, t, d) if mixed and dst == 'bm' else (n * t, d), F32)]
    out_specs = [_tile_spec(dst, mixed, n, d)]
    scratch = []
    if proj == 'pool':
        assert x_layout == 'tm' and ROW_TILE % n == 0 and n & (n - 1) == 0
        g_mix, tail0, w_pool, scale = mixer[:-1]
        operands = [g_mix, tail0, w_pool, scale]
        specs = [_layer_spec(g_mix.shape, layer), _const_spec(tail0.shape), _layer_spec(w_pool.shape, j),
                 _layer_spec(scale.shape, j)]
        out_shape.append(jax.ShapeDtypeStruct(tail0.shape, F32))
        out_specs.append(pl.BlockSpec(tail0.shape, lambda i: (0, 0)))
        scratch.append(pltpu.VMEM(tail0.shape, F32))
    elif proj is None:
        operands, specs = [], []
    else:
        rows_in, w_proj = mixer[:-2], mixer[-2]
        operands = [_tile_view(z, a_layout, mixed, n, t) for z in rows_in] + [w_proj]
        specs = [_tile_spec(a_layout, mixed, n, z.shape[1]) for z in rows_in] + [_const_spec(w_proj.shape)]
    operands += [_tile_view(x, x_layout, mixed, n, t), g, w_up, w_down, g_final.reshape(1, d)]
    specs += [_tile_spec(x_layout, mixed, n, d), _layer_spec(g.shape, layer), _const_spec(w_up.shape),
              _const_spec(w_down.shape), _const_spec((1, d))]
    stream = n * t == ROW_TILE and proj != 'pool' and not mixed
    grid = (n * t // ROW_TILE,)
    if stream:
        d_ff = w_up.shape[1]
        grid = (d_ff // MLP_FF_CHUNK,)
        fixed_rows = lambda spec: pl.BlockSpec(spec.block_shape, lambda c: (0, 0))
        n_mixer_rows = len(specs) - 5 - (proj is not None)
        for k in list(range(n_mixer_rows)) + [len(specs) - 5]:
            specs[k] = fixed_rows(specs[k])
        specs[-3] = pl.BlockSpec((d, MLP_FF_CHUNK), lambda c: (0, c))
        specs[-2] = pl.BlockSpec((MLP_FF_CHUNK, d), lambda c: (c, 0))
        out_specs = [fixed_rows(out_specs[0])]
        scratch = [pltpu.VMEM((ROW_TILE, d), F32), pltpu.VMEM((ROW_TILE, d), BF16), pltpu.VMEM((ROW_TILE, d), F32)]
    outs = pl.pallas_call(
        functools.partial(_mlp_body, proj=proj, final=final, ff_chunk=MLP_FF_CHUNK, n=n, pos0=pos0, stream=stream,
                          mix_to=x_layout if proj in ('glu', 'gated') and a_layout != x_layout else None,
                          out_to=dst if dst != x_layout else None),
        grid=grid,
        in_specs=specs,
        out_specs=out_specs,
        out_shape=out_shape,
        scratch_shapes=scratch,
        compiler_params=_params(1),
        name=(proj + "_mlp" if proj else "mlp") + ("_final" if final else ""),
    )(*operands)
    return (outs[0].reshape(n * t, d),) + tuple(outs[1:])


def _pool_body(g_ref, buf_ref, w_ref, sc_ref, x_ref, o_ref, tail_out, tail, *, n, pos0):
    x = x_ref[...]
    o_ref[...] = x + _pool_mix(x, g_ref, buf_ref, w_ref, sc_ref, tail_out, tail, n=n, pos0=pos0)


def _pool(x, g_mix, layer, tail0, w_pool, scale, j, n, pos0):
    rows, d = x.shape
    assert ROW_TILE % n == 0 and n & (n - 1) == 0
    return pl.pallas_call(
        functools.partial(_pool_body, n=n, pos0=pos0),
        grid=(rows // ROW_TILE,),
        in_specs=[_layer_spec(g_mix.shape, layer), _const_spec(tail0.shape), _layer_spec(w_pool.shape, j),
                  _layer_spec(scale.shape, j), pl.BlockSpec((ROW_TILE, d), lambda i: (i, 0))],
        out_specs=[pl.BlockSpec((ROW_TILE, d), lambda i: (i, 0)), pl.BlockSpec(tail0.shape, lambda i: (0, 0))],
        out_shape=[jax.ShapeDtypeStruct((rows, d), F32), jax.ShapeDtypeStruct(tail0.shape, F32)],
        scratch_shapes=[pltpu.VMEM(tail0.shape, F32)],
        compiler_params=_params(1),
        name="pool_mixer",
    )(g_mix, tail0, w_pool, scale, x)


def _cast_specs(jobs, steps):
    operands, in_specs, out_specs, out_shapes = [], [], [], []
    for w, layer in jobs:
        _, rows, cols = w.shape
        assert rows % (steps * BF16_ROWS) == 0
        blk = rows // steps
        operands.append(w)
        in_specs.append(pl.BlockSpec((None, blk, cols), lambda i, layer=layer: (layer, i, 0)))
        out_specs.append(pl.BlockSpec((blk, cols), lambda i: (i, 0)))
        out_shapes.append(jax.ShapeDtypeStruct((rows, cols), BF16))
    return operands, in_specs, out_specs, out_shapes


def _run_casts(src_refs, dst_refs):
    for src, dst in zip(src_refs, dst_refs, strict=True):
        dst[...] = src[...].astype(BF16)


def _s5_body(*refs, nb, tc, n_cast):
    x_ref, g_ref, bw_ref, are_ref, aim_ref, cre_ref, cim_ref, dsk_ref, s0re_ref, s0im_ref = refs[:10]
    cast_src, refs = refs[10:10 + n_cast], refs[10 + n_cast:]
    gl_ref, sre_out, sim_out = refs[:3]
    cast_dst, (bu_re, bu_im, st_re, st_im) = refs[3:3 + n_cast], refs[3 + n_cast:]
    _run_casts(cast_src, cast_dst)
    i = pl.program_id(0)

    @pl.when(i == 0)
    def _():
        st_re[...] = s0re_ref[...]
        st_im[...] = s0im_ref[...]

    d = x_ref.shape[-1]
    x = x_ref[...].reshape(tc * nb, d)
    if len(x_ref.shape) == 3:
        x = _permute_rows(x, nb, 'tm')
    h = _rms(x, g_ref[...])
    hb = h.astype(BF16)
    n_state_slab = bw_ref.shape[2] // 2
    ys = []
    for j in range(bw_ref.shape[0]):
        r = jnp.dot(hb[:, j * MXU_DIM:(j + 1) * MXU_DIM], bw_ref[j], preferred_element_type=F32)
        cols = slice(j * n_state_slab, (j + 1) * n_state_slab)
        bu_re[:, cols] = r[:, :n_state_slab]
        bu_im[:, cols] = r[:, n_state_slab:]
        for c in range(n_state_slab // SCAN_LANES):
            lanes = slice(j * n_state_slab + c * SCAN_LANES, j * n_state_slab + (c + 1) * SCAN_LANES)
            ar = jnp.broadcast_to(are_ref[:, lanes], (SUBLANES, SCAN_LANES))
            ai = jnp.broadcast_to(aim_ref[:, lanes], (SUBLANES, SCAN_LANES))
            for rg in range(nb // SUBLANES):
                seqs = slice(rg * SUBLANES, (rg + 1) * SUBLANES)
                sr, si = st_re[seqs, lanes], st_im[seqs, lanes]
                for t in range(tc):
                    rows = slice(t * nb + rg * SUBLANES, t * nb + (rg + 1) * SUBLANES)
                    sr, si = (ar * sr - ai * si + bu_re[rows, lanes], ar * si + ai * sr + bu_im[rows, lanes])
                    bu_re[rows, lanes] = sr
                    bu_im[rows, lanes] = si
                st_re[seqs, lanes] = sr
                st_im[seqs, lanes] = si
        y = jnp.dot(bu_re[:, cols].astype(BF16), cre_ref[j], preferred_element_type=F32)
        ys.append(y + jnp.dot(bu_im[:, cols].astype(BF16), cim_ref[j], preferred_element_type=F32))
    y = jnp.concatenate(ys, axis=1) + dsk_ref[...] * h
    gl_ref[...] = jax.nn.gelu(y, approximate=True).astype(BF16)

    @pl.when(i == pl.num_programs(0) - 1)
    def _():
        sre_out[...] = st_re[...]
        sim_out[...] = st_im[...]


def _s5_core(x, x_layout, g, layer, packed, d_skip, j, s0_re, s0_im, nb, t, tc, cast_jobs):
    rows, d = x.shape
    bw, a_re, a_im, c_re, c_im = packed
    n_lanes = a_re.shape[-1]
    r = tc * nb
    from_bm = x_layout == 'bm'
    assert not from_bm or (r == ROW_TILE and _can_relayout_in_kernel(nb, t))
    cast_ops, cast_in, cast_out, cast_shapes = _cast_specs(cast_jobs, rows // r)
    return pl.pallas_call(
        functools.partial(_s5_body, nb=nb, tc=tc, n_cast=len(cast_jobs)),
        grid=(rows // r,),
        in_specs=[_tile_spec(x_layout, from_bm, nb, d) if from_bm else pl.BlockSpec((r, d), lambda i: (i, 0)),
                  _layer_spec(g.shape, layer), _layer_spec(bw.shape, j), _layer_spec(a_re.shape, j),
                  _layer_spec(a_im.shape, j), _layer_spec(c_re.shape, j), _layer_spec(c_im.shape, j),
                  _layer_spec(d_skip.shape, j),
                  _const_spec((nb, n_lanes)), _const_spec((nb, n_lanes))] + cast_in,
        out_specs=[pl.BlockSpec((r, d), lambda i: (i, 0)),
                   pl.BlockSpec((nb, n_lanes), lambda i: (0, 0)), pl.BlockSpec((nb, n_lanes), lambda i: (0, 0))]
                  + cast_out,
        out_shape=[jax.ShapeDtypeStruct((rows, d), BF16),
                   jax.ShapeDtypeStruct((nb, n_lanes), F32), jax.ShapeDtypeStruct((nb, n_lanes), F32)] + cast_shapes,
        scratch_shapes=[pltpu.VMEM((r, n_lanes), F32), pltpu.VMEM((r, n_lanes), F32),
                        pltpu.VMEM((nb, n_lanes), F32), pltpu.VMEM((nb, n_lanes), F32)],
        compiler_params=_params(1),
        name="s5_core",
    )(_tile_view(x, x_layout, from_bm, nb, t), g, bw, a_re, a_im, c_re, c_im, d_skip, s0_re, s0_im, *cast_ops)


def _s5_weights(a_re, a_im, log_dt, b_re, b_im, c_re, c_im):
    groups, p, hch = b_re.shape
    dt = jnp.exp(log_dt.astype(F32))[:, None]
    lr, li = a_re.astype(F32), a_im.astype(F32)
    mag = jnp.exp(lr * dt)
    ab_re, ab_im = mag * jnp.cos(li * dt), mag * jnp.sin(li * dt)
    den = lr * lr + li * li
    nr, ni = ab_re - 1.0, ab_im
    f_re = (nr * lr + ni * li) / den
    f_im = (ni * lr - nr * li) / den
    br, bi = b_re.astype(F32), b_im.astype(F32)
    bb_re = f_re[..., None] * br - f_im[..., None] * bi
    bb_im = f_re[..., None] * bi + f_im[..., None] * br
    gs = MXU_DIM // hch
    n_slab = groups // gs

    def block_diag(per_group):
        r, c = per_group.shape[1:]
        tiled = jnp.tile(per_group.reshape(n_slab, gs * r, c), (1, 1, gs))
        row_g = lax.broadcasted_iota(jnp.int32, (gs * r, gs * c), 0) // r
        col_g = lax.broadcasted_iota(jnp.int32, (gs * r, gs * c), 1) // c
        return jnp.where(row_g == col_g, tiled, 0.0).astype(BF16)

    to_hp = lambda bb: bb.transpose(0, 2, 1)
    bw = jnp.concatenate([block_diag(to_hp(bb_re)), block_diag(to_hp(bb_im))], axis=2)
    cw_re = block_diag(c_re.astype(F32).transpose(0, 2, 1))
    cw_im = block_diag(-c_im.astype(F32).transpose(0, 2, 1))
    return bw, ab_re.reshape(1, groups * p), ab_im.reshape(1, groups * p), cw_re, cw_im


def _ret_in_body(*refs, d, qk, n_cast):
    x_ref, g_ref, w_ref, cos_ref, sin_ref = refs[:5]
    q_ref, k_ref, v_ref, gate_ref = refs[5 + n_cast:9 + n_cast]
    _run_casts(refs[5:5 + n_cast], refs[9 + n_cast:])
    h = _rms(x_ref[...], g_ref[...]).astype(BF16)
    cos, sin = cos_ref[...], sin_ref[...]
    half = qk // 2

    def rot(z):
        outs = []
        for hd in range(d // qk):
            z1 = z[:, hd * qk:hd * qk + half]
            z2 = z[:, hd * qk + half:(hd + 1) * qk]
            outs += [z1 * cos - z2 * sin, z1 * sin + z2 * cos]
        return jnp.concatenate(outs, axis=1)

    q = jnp.dot(h, w_ref[:, 0:d], preferred_element_type=F32)
    q_ref[...] = (rot(q) * (qk ** -0.5)).astype(BF16)
    k = jnp.dot(h, w_ref[:, d:2 * d], preferred_element_type=F32)
    k_ref[...] = rot(k).astype(BF16)
    v_ref[...] = jnp.dot(h, w_ref[:, 2 * d:4 * d], preferred_element_type=F32).astype(BF16)
    gate_ref[...] = jnp.dot(h, w_ref[:, 4 * d:6 * d], preferred_element_type=F32).astype(BF16)


def _ret_in(x, g, layer, w_in, cos_tab, sin_tab, cast_jobs):
    rows, d = x.shape
    qk = d // RET_HEADS
    tab_blocks = cos_tab.shape[0] // ROW_TILE
    row_spec = lambda width: pl.BlockSpec((ROW_TILE, width), lambda i: (i, 0))
    tab_spec = pl.BlockSpec((ROW_TILE, qk // 2), lambda i: (i % tab_blocks, 0))
    cast_ops, cast_in, cast_out, cast_shapes = _cast_specs(cast_jobs, rows // ROW_TILE)
    return pl.pallas_call(
        functools.partial(_ret_in_body, d=d, qk=qk, n_cast=len(cast_jobs)),
        grid=(rows // ROW_TILE,),
        in_specs=[row_spec(d), _layer_spec(g.shape, layer), _const_spec(w_in.shape), tab_spec, tab_spec] + cast_in,
        out_specs=[row_spec(d), row_spec(d), row_spec(2 * d), row_spec(2 * d)] + cast_out,
        out_shape=[jax.ShapeDtypeStruct((rows, d), BF16), jax.ShapeDtypeStruct((rows, d), BF16),
                   jax.ShapeDtypeStruct((rows, 2 * d), BF16), jax.ShapeDtypeStruct((rows, 2 * d), BF16)] + cast_shapes,
        compiler_params=_params(1),
        name="ret_in",
    )(x, g, w_in, cos_tab, sin_tab, *cast_ops)


def _ret_core_body(q_ref, k_ref, v_ref, dmask_ref, qdec_ref, kdec_ref, cdec_ref, *rest,
                   lb, lp, qk, vd, seqs, chunks):
    o_ref, s_ref = rest[-2:]
    c = pl.program_id(1)

    @pl.when(c == 0)
    def _():
        s_ref[...] = rest[0][...] if len(rest) == 3 else jnp.zeros(s_ref.shape, s_ref.dtype)

    def padded(z):
        if lb == lp:
            return z
        return jnp.concatenate([z, jnp.zeros((lp - lb, z.shape[1]), z.dtype)], axis=0)

    for blk in range(seqs * chunks):
        b = blk // chunks
        rows = slice(blk * lb, (blk + 1) * lb)
        q, k, v = padded(q_ref[rows, :]), padded(k_ref[rows, :]), padded(v_ref[rows, :])
        for hd in range(RET_HEADS):
            qh = q[:, hd * qk:(hd + 1) * qk]
            kh = k[:, hd * qk:(hd + 1) * qk]
            kf = kh.astype(F32)
            vh = v[:, hd * vd:(hd + 1) * vd]
            s_old = s_ref[b, hd]
            scores = lax.dot_general(qh, kh, (((1,), (1,)), ((), ())), preferred_element_type=F32) * dmask_ref[hd]
            o = jnp.dot(scores.astype(BF16), vh, preferred_element_type=F32)
            o = o + qdec_ref[hd] * jnp.dot(qh, s_old.astype(BF16), preferred_element_type=F32)
            kd = (kf * kdec_ref[hd]).astype(BF16)
            s_ref[b, hd] = cdec_ref[hd] * s_old + lax.dot_general(kd, vh, (((0,), (0,)), ((), ())),
                                                                  preferred_element_type=F32)
            o_ref[rows, hd * vd:(hd + 1) * vd] = o[:lb]


def _ret_core(q, k, v, s0, consts, n, nc, lb):
    rows, d = q.shape
    qk, vd = d // RET_HEADS, 2 * d // RET_HEADS
    dmask, qdec, kdec, cdec = consts
    lp = dmask.shape[1]
    chunks = math.gcd(nc, max(ROW_TILE // lb, 1))
    seqs = math.gcd(n, RET_SEQS_PER_STEP) if chunks == nc else 1
    nc_blocks = nc // chunks
    row_spec = lambda width: pl.BlockSpec((seqs * chunks * lb, width), lambda b, c: (b * nc_blocks + c, 0))
    s_spec = pl.BlockSpec((seqs, RET_HEADS, qk, vd), lambda b, c: (b, 0, 0, 0))
    return pl.pallas_call(
        functools.partial(_ret_core_body, lb=lb, lp=lp, qk=qk, vd=vd, seqs=seqs, chunks=chunks),
        grid=(n // seqs, nc_blocks),
        in_specs=[row_spec(d), row_spec(d), row_spec(2 * d),
                  _const_spec(dmask.shape), _const_spec(qdec.shape), _const_spec(kdec.shape), _const_spec(cdec.shape)]
                 + ([] if s0 is None else [s_spec]),
        out_specs=[row_spec(2 * d), s_spec],
        out_shape=[jax.ShapeDtypeStruct((rows, 2 * d), F32), jax.ShapeDtypeStruct((n, RET_HEADS, qk, vd), F32)],
        compiler_params=_params(2),
        name="ret_core",
    )(q, k, v, dmask, qdec, kdec, cdec, *(() if s0 is None else (s0,)))


def _ret_consts(l_real, l_pad, qk, vd):
    log_g = jnp.log1p(-jnp.exp2(-5.0 - jnp.arange(RET_HEADS, dtype=F32)))
    idx = jnp.arange(l_real, dtype=F32)
    diff = idx[:, None] - idx[None, :]
    dmask = jnp.exp(jnp.maximum(diff, 0.0)[None] * log_g[:, None, None]) * (diff >= 0)[None]
    q_dec = jnp.exp((idx + 1.0)[:, None] * log_g[None, :])
    k_dec = jnp.exp((l_real - 1.0 - idx)[:, None] * log_g[None, :])
    chunk_dec = jnp.exp(l_real * log_g)
    pad = l_pad - l_real
    dmask = jnp.pad(dmask, ((0, 0), (0, pad), (0, pad)))
    q_dec = jnp.pad(q_dec.T, ((0, 0), (0, pad)))
    k_dec = jnp.pad(k_dec.T, ((0, 0), (0, pad)))
    qdec = jnp.broadcast_to(q_dec[:, :, None], (RET_HEADS, l_pad, vd))
    kdec = jnp.broadcast_to(k_dec[:, :, None], (RET_HEADS, l_pad, qk))
    cdec = jnp.broadcast_to(chunk_dec[:, None, None], (RET_HEADS, 1, vd))
    return dmask, qdec, kdec, cdec


def _rope_tables(pos0, t, qk):
    half = qk // 2
    inv = ROPE_BASE ** (-jnp.arange(half, dtype=F32) / half)
    pos = pos0 + jnp.arange(t)
    ang = pos.astype(F32)[:, None] * inv[None, :]
    return jnp.cos(ang), jnp.sin(ang)


def _relayout(x, n, t, src, dst):
    if src == dst:
        return x
    lead = (n, t) if src == 'bm' else (t, n)
    return x.reshape(lead + (-1,)).transpose(1, 0, 2).reshape(n * t, -1)


def _mixer_layout(layer):
    return 'bm' if layer % 3 == 1 else 'tm'


def _cast_jobs(wts, layer, steps):
    depth = wts['mlp_w_up'].shape[0]
    mlp_layers = [layer]
    while mlp_layers[-1] + 1 < depth and (mlp_layers[-1] + 1) % 3 == 2:
        mlp_layers.append(mlp_layers[-1] + 1)
    keys = [(name, l) for l in mlp_layers for name in ('mlp_w_up', 'mlp_w_down')]
    if layer % 3 == 0:
        keys.append(('s5_w_glu', layer // 3))
        if layer + 1 < depth and (layer + 1) % 3 == 1:
            keys.append(('ret_w_in', (layer + 1) // 3))
    elif layer % 3 == 1:
        keys.append(('ret_w_out', layer // 3))
    keys = [k for k in keys if k not in wts['bf16'] and wts[k[0]].shape[1] % (steps * BF16_ROWS) == 0]
    return keys, [(wts[name], idx) for name, idx in keys]


def _bf16_weight(wts, name, idx):
    if (name, idx) not in wts['bf16']:
        wts['bf16'][(name, idx)] = wts[name][idx].astype(BF16)
    return wts['bf16'][(name, idx)]


def _trunk(x, pos0, st_re, st_im, st_ret, st_pool, wts):
    n, t, d = x.shape
    rows = n * t
    depth = wts['norm_mix'].shape[0]
    qk, vd = d // RET_HEADS, 2 * d // RET_HEADS
    tc = min(max(ROW_TILE // n, 1), t)
    in_kernel_relayout = _can_relayout_in_kernel(n, t)
    new_re, new_im, new_ret, new_pool = [], [], [], []

    xb, layout = x.reshape(rows, d), 'bm'
    for layer in range(depth):
        kind, j = layer % 3, layer // 3
        want = _mixer_layout(layer)
        if layout != want and not (kind == 0 and in_kernel_relayout):
            xb, layout = _relayout(xb, n, t, layout, want), want
        if kind == 1:
            l_real = math.gcd(t, RET_CHUNK)
            nc = t // l_real
            lb = max(l_real, BF16_ROWS)
            cos, sin = _rope_tables(pos0, t, qk)
            if t < ROW_TILE:
                cos, sin = jnp.tile(cos, (ROW_TILE // t, 1)), jnp.tile(sin, (ROW_TILE // t, 1))
            cast_keys, cast_jobs = _cast_jobs(wts, layer, rows // ROW_TILE)
            q, k, v, gate, *casts = _ret_in(xb, wts['norm_mix'], layer, _bf16_weight(wts, 'ret_w_in', j), cos, sin,
                                            cast_jobs)
            if lb != l_real:
                pad = lambda z: jnp.pad(z.reshape(n * nc, l_real, -1), ((0, 0), (0, lb - l_real), (0, 0))
                                        ).reshape(n * nc * lb, -1)
                q, k, v = pad(q), pad(k), pad(v)
            consts = _ret_consts(l_real, max(l_real, RET_MIN_CHUNK_ROWS), qk, vd)
            o, s_new = _ret_core(q, k, v, None if st_ret is None else st_ret[j], consts, n, nc, lb)
            if lb != l_real:
                o = o.reshape(n * nc, lb, -1)[:, :l_real].reshape(rows, -1)
            proj, mixer = 'gated', (o, gate, ('ret_w_out', j), j)
            new_ret.append(s_new)
        elif kind == 0:
            cast_keys, cast_jobs = _cast_jobs(wts, layer, rows // (tc * n))
            gl, s_re, s_im, *casts = _s5_core(xb, layout, wts['norm_mix'], layer, wts['s5_packed'], wts['s5_d'], j,
                                              st_re[j].reshape(n, -1), st_im[j].reshape(n, -1), n, t, tc, cast_jobs)
            proj, mixer = 'glu', (gl, ('s5_w_glu', j), j)
            new_re.append(s_re.reshape(st_re[j].shape))
            new_im.append(s_im.reshape(st_im[j].shape))
        else:
            cast_keys, casts = [], []
            buf = st_pool[j]
            buf_t = buf.transpose(1, 0, 2).reshape(POOL_TAIL * n, d)
            if buf_t.size * buf_t.dtype.itemsize <= POOL_FUSE_MAX_TAIL_BYTES:
                proj, mixer = 'pool', (wts['norm_mix'], buf_t, wts['pool_w'], wts['pool_scale'], j)
            else:
                xb, tail = _pool(xb, wts['norm_mix'], layer, buf_t, wts['pool_w'], wts['pool_scale'], j, n, pos0)
                new_pool.append(tail.reshape(POOL_TAIL, n, d).transpose(1, 0, 2))
                proj, mixer = None, ()
        wts['bf16'].update(zip(cast_keys, casts, strict=True))
        if proj in ('glu', 'gated'):
            mixer = mixer[:-2] + (_bf16_weight(wts, *mixer[-2]), j)
        final = layer == depth - 1
        wanted = 'bm' if final else _mixer_layout(layer + 1)
        dst = wanted if in_kernel_relayout else layout
        xb, *tail = _mlp(proj, mixer, want, xb, layout, wts['norm_ffn'], _bf16_weight(wts, 'mlp_w_up', layer),
                         _bf16_weight(wts, 'mlp_w_down', layer), layer, wts['norm_final'], final, n, t, dst, pos0)
        if tail:
            new_pool.append(tail[0].reshape(POOL_TAIL, n, d).transpose(1, 0, 2))
        layout = dst
    y = _relayout(xb, n, t, layout, 'bm').reshape(n, t, d)
    return y, jnp.stack(new_re), jnp.stack(new_im), jnp.stack(new_ret), jnp.stack(new_pool)


def kernel(x_prompt, x_sample, state_s5_re, state_s5_im, state_ret, state_pool, norm_mix, norm_ffn, norm_final,
           s5_a_re, s5_a_im, s5_log_dt, s5_b_re, s5_b_im, s5_c_re, s5_c_im, s5_d, s5_w_glu, ret_w_in, ret_w_out,
           pool_w, pool_scale, mlp_w_up, mlp_w_down):
    nb = x_prompt.shape[0]
    depth, d = norm_mix.shape
    wts = dict(
        norm_mix=norm_mix.reshape(depth, 1, d), norm_ffn=norm_ffn.reshape(depth, 1, d), norm_final=norm_final,
        s5_d=s5_d.reshape(s5_d.shape[0], 1, d),
        s5_packed=jax.vmap(_s5_weights)(s5_a_re, s5_a_im, s5_log_dt, s5_b_re, s5_b_im, s5_c_re, s5_c_im),
        s5_w_glu=s5_w_glu, ret_w_in=ret_w_in, ret_w_out=ret_w_out, mlp_w_up=mlp_w_up, mlp_w_down=mlp_w_down,
        bf16={},
        pool_w=pool_w.astype(BF16), pool_scale=pool_scale.reshape(pool_scale.shape[0], 1, d))
    z_re = jnp.zeros((state_s5_re.shape[0], nb) + state_s5_re.shape[2:], state_s5_re.dtype)
    z_im = jnp.zeros((state_s5_im.shape[0], nb) + state_s5_im.shape[2:], state_s5_im.dtype)
    z_pool = jnp.zeros((state_pool.shape[0], nb) + state_pool.shape[2:], state_pool.dtype)
    y_p, p_re, p_im, p_ret, p_pool = _trunk(x_prompt, 0, z_re, z_im, None, z_pool, wts)
    y_s, s_re, s_im, s_ret, s_pool = _trunk(x_sample, PAST_LEN, state_s5_re, state_s5_im, state_ret, state_pool, wts)
    return (y_p, y_s, p_re, p_im, p_ret, p_pool, s_re, s_im, s_ret, s_pool)
```

```python
import functools
import math

import jax
import jax.numpy as jnp
from jax import lax
from jax.experimental import pallas as pl
from jax.experimental.pallas import tpu as pltpu

F32 = jnp.float32
BF16 = jnp.bfloat16

EPS = 1e-6
RET_HEADS = 4
RET_CHUNK = 256
RET_MIN_CHUNK_ROWS = 16
ROPE_BASE = 10000.0
POOL_WINDOWS = (2, 4, 8, 16)
POOL_TAIL = max(POOL_WINDOWS) - 1

SUBLANES = 8
BF16_ROWS = 16
MXU_DIM = 256
ROW_TILE = 512
MLP_FF_CHUNK = 1024
SCAN_LANES = 512
RET_SEQS_PER_STEP = 4
POOL_FUSE_MAX_TAIL_BYTES = 1 << 20
VMEM_LIMIT = 56 * 1024 * 1024
PAST_LEN = 16384


def _params(n_axes):
    return pltpu.CompilerParams(dimension_semantics=("arbitrary",) * n_axes, vmem_limit_bytes=VMEM_LIMIT)


def _const_spec(shape):
    zeros = (0,) * len(shape)
    return pl.BlockSpec(shape, lambda *_: zeros, pipeline_mode=pl.Buffered(1))


def _layer_spec(stacked_shape, layer):
    idx = (layer,) + (0,) * (len(stacked_shape) - 1)
    return pl.BlockSpec((None,) + tuple(stacked_shape[1:]), lambda *_: idx, pipeline_mode=pl.Buffered(1))


def _can_relayout_in_kernel(n, t):
    steps = ROW_TILE // n
    return ROW_TILE % n == 0 and steps % BF16_ROWS == 0 and t % steps == 0


def _tile_spec(layout, mixed, n, width):
    if mixed and layout == 'bm':
        return pl.BlockSpec((n, ROW_TILE // n, width), lambda i: (0, i, 0))
    return pl.BlockSpec((ROW_TILE, width), lambda i: (i, 0))


def _tile_view(x, layout, mixed, n, t):
    return x.reshape(n, t, x.shape[-1]) if mixed and layout == 'bm' else x


def _permute_rows(y, n, to):
    rows, d = y.shape
    lead = (rows // n, n) if to == 'bm' else (n, rows // n)
    return jnp.swapaxes(y.reshape(lead + (d,)), 0, 1).reshape(rows, d)


def _rms(x, g):
    ms = jnp.mean(x * x, axis=-1, keepdims=True)
    return x * lax.rsqrt(ms + EPS) * g


def _gated_heads_proj(o, gate, wp_ref):
    vd = o.shape[1] // RET_HEADS
    mix = None
    for hd in range(RET_HEADS):
        cols = slice(hd * vd, (hd + 1) * vd)
        oc = o[:, cols] - jnp.mean(o[:, cols], axis=-1, keepdims=True)
        on = oc * lax.rsqrt(jnp.mean(oc * oc, axis=-1, keepdims=True) + EPS)
        og = (on * jax.nn.silu(gate[:, cols].astype(F32))).astype(BF16)
        part = jnp.dot(og, wp_ref[cols, :], preferred_element_type=F32)
        mix = part if mix is None else mix + part
    return mix


def _pool_mix(x, g_ref, buf_ref, w_ref, sc_ref, tail_out, tail, *, n, pos0):
    i = pl.program_id(0)

    @pl.when(i == 0)
    def _():
        tail[...] = buf_ref[...]

    rows, d = x.shape
    tc = rows // n
    h = _rms(x, g_ref[...])
    ext = jnp.concatenate([tail[...], h], axis=0)
    grp = d // len(POOL_WINDOWS)
    t_idx = i * tc + lax.shift_right_logical(lax.broadcasted_iota(jnp.int32, (rows, grp), 0), n.bit_length() - 1)
    zs = []
    for gi, w in enumerate(POOL_WINDOWS):
        cols = slice(gi * grp, (gi + 1) * grp)
        s, first, span = ext[:, cols], 0, 1
        while span < w:
            s = s[span * n:] + s[:-span * n]
            first += span
            span *= 2
        s = s[(POOL_TAIL - first) * n:]
        cnt = jnp.minimum(w, pos0 + t_idx + 1).astype(F32)
        pooled = s / cnt - h[:, cols]
        zs.append(jnp.dot(pooled.astype(BF16), w_ref[gi], preferred_element_type=F32))
    new_tail = ext[rows:]
    tail[...] = new_tail

    @pl.when(i == pl.num_programs(0) - 1)
    def _():
        tail_out[...] = new_tail

    return jnp.concatenate(zs, axis=1) * sc_ref[...]


MIXER_REFS = {None: 0, 'glu': 2, 'gated': 3, 'pool': 4}


def _mlp_body(*refs, proj, final, ff_chunk, n, mix_to, out_to, pos0, stream):
    mix_refs = refs[:MIXER_REFS[proj]]
    x_ref, g_ref, wu_ref, wd_ref, gf_ref, o_ref = refs[len(mix_refs):len(mix_refs) + 6]
    d = x_ref.shape[-1]

    def residual():
        x = x_ref[...].reshape(ROW_TILE, d)
        if proj == 'pool':
            return x + _pool_mix(x, *mix_refs, *refs[len(mix_refs) + 6:], n=n, pos0=pos0)
        if proj is None:
            return x
        a = mix_refs[0][...].reshape(ROW_TILE, mix_refs[0].shape[-1])
        if proj == 'glu':
            mix = jnp.dot(a, mix_refs[1][...], preferred_element_type=F32)
            mix = mix[:, :d] * jax.nn.sigmoid(mix[:, d:])
        else:
            mix = _gated_heads_proj(a, mix_refs[1][...].reshape(a.shape), mix_refs[2])
        if mix_to is not None:
            mix = _permute_rows(mix, n, mix_to)
        return x + mix

    def hidden_chunk(h, wu, wd):
        a = jnp.dot(h, wu, preferred_element_type=F32)
        return jnp.dot(jnp.square(jnp.maximum(a, 0.0)).astype(BF16), wd, preferred_element_type=F32)

    def finish(x, acc):
        y = x + acc
        if final:
            y = _rms(y, gf_ref[...])
        if out_to is not None:
            y = _permute_rows(y, n, out_to)
        o_ref[...] = y.reshape(o_ref.shape)

    if not stream:
        x = residual()
        h = _rms(x, g_ref[...]).astype(BF16)
        acc = jnp.zeros_like(x)
        for c in range(wu_ref.shape[1] // ff_chunk):
            cols = slice(c * ff_chunk, (c + 1) * ff_chunk)
            acc = acc + hidden_chunk(h, wu_ref[:, cols], wd_ref[cols, :])
        finish(x, acc)
        return

    x1, hb, acc = refs[-3:]
    c = pl.program_id(0)

    @pl.when(c == 0)
    def _():
        x = residual()
        x1[...] = x
        hb[...] = _rms(x, g_ref[...]).astype(BF16)
        acc[...] = jnp.zeros_like(x)

    acc[...] += hidden_chunk(hb[...], wu_ref[...], wd_ref[...])

    @pl.when(c == pl.num_programs(0) - 1)
    def _():
        finish(x1[...], acc[...])


def _mlp(proj, mixer, a_layout, x, x_layout, g, w_up, w_down, layer, g_final, final, n, t, dst, pos0):
    d = x.shape[1]
    j = mixer[-1] if mixer else None
    mixed = len({x_layout, dst} | ({a_layout} if proj in ('glu', 'gated') else set())) > 1
    assert not mixed or _can_relayout_in_kernel(n, t)
    out_shape = [jax.ShapeDtypeStruct((n, t, d) if mixed and dst == 'bm' else (n * t, d), F32)]
    out_specs = [_tile_spec(dst, mixed, n, d)]
    scratch = []
    if proj == 'pool':
        assert x_layout == 'tm' and ROW_TILE % n == 0 and n & (n - 1) == 0
        g_mix, tail0, w_pool, scale = mixer[:-1]
        operands = [g_mix, tail0, w_pool, scale]
        specs = [_layer_spec(g_mix.shape, layer), _const_spec(tail0.shape), _layer_spec(w_pool.shape, j),
                 _layer_spec(scale.shape, j)]
        out_shape.append(jax.ShapeDtypeStruct(tail0.shape, F32))
        out_specs.append(pl.BlockSpec(tail0.shape, lambda i: (0, 0)))
        scratch.append(pltpu.VMEM(tail0.shape, F32))
    elif proj is None:
        operands, specs = [], []
    else:
        rows_in, w_proj = mixer[:-2], mixer[-2]
        operands = [_tile_view(z, a_layout, mixed, n, t) for z in rows_in] + [w_proj]
        specs = [_tile_spec(a_layout, mixed, n, z.shape[1]) for z in rows_in] + [_const_spec(w_proj.shape)]
    operands += [_tile_view(x, x_layout, mixed, n, t), g, w_up, w_down, g_final.reshape(1, d)]
    specs += [_tile_spec(x_layout, mixed, n, d), _layer_spec(g.shape, layer), _const_spec(w_up.shape),
              _const_spec(w_down.shape), _const_spec((1, d))]
    stream = n * t == ROW_TILE and proj != 'pool' and not mixed
    grid = (n * t // ROW_TILE,)
    if stream:
        d_ff = w_up.shape[1]
        grid = (d_ff // MLP_FF_CHUNK,)
        fixed_rows = lambda spec: pl.BlockSpec(spec.block_shape, lambda c: (0, 0))
        n_mixer_rows = len(specs) - 5 - (proj is not None)
        for k in list(range(n_mixer_rows)) + [len(specs) - 5]:
            specs[k] = fixed_rows(specs[k])
        specs[-3] = pl.BlockSpec((d, MLP_FF_CHUNK), lambda c: (0, c))
        specs[-2] = pl.BlockSpec((MLP_FF_CHUNK, d), lambda c: (c, 0))
        out_specs = [fixed_rows(out_specs[0])]
        scratch = [pltpu.VMEM((ROW_TILE, d), F32), pltpu.VMEM((ROW_TILE, d), BF16), pltpu.VMEM((ROW_TILE, d), F32)]
    outs = pl.pallas_call(
        functools.partial(_mlp_body, proj=proj, final=final, ff_chunk=MLP_FF_CHUNK, n=n, pos0=pos0, stream=stream,
                          mix_to=x_layout if proj in ('glu', 'gated') and a_layout != x_layout else None,
                          out_to=dst if dst != x_layout else None),
        grid=grid,
        in_specs=specs,
        out_specs=out_specs,
        out_shape=out_shape,
        scratch_shapes=scratch,
        compiler_params=_params(1),
        name=(proj + "_mlp" if proj else "mlp") + ("_final" if final else ""),
    )(*operands)
    return (outs[0].reshape(n * t, d),) + tuple(outs[1:])


def _pool_body(g_ref, buf_ref, w_ref, sc_ref, x_ref, o_ref, tail_out, tail, *, n, pos0):
    x = x_ref[...]
    o_ref[...] = x + _pool_mix(x, g_ref, buf_ref, w_ref, sc_ref, tail_out, tail, n=n, pos0=pos0)


def _pool(x, g_mix, layer, tail0, w_pool, scale, j, n, pos0):
    rows, d = x.shape
    assert ROW_TILE % n == 0 and n & (n - 1) == 0
    return pl.pallas_call(
        functools.partial(_pool_body, n=n, pos0=pos0),
        grid=(rows // ROW_TILE,),
        in_specs=[_layer_spec(g_mix.shape, layer), _const_spec(tail0.shape), _layer_spec(w_pool.shape, j),
                  _layer_spec(scale.shape, j), pl.BlockSpec((ROW_TILE, d), lambda i: (i, 0))],
        out_specs=[pl.BlockSpec((ROW_TILE, d), lambda i: (i, 0)), pl.BlockSpec(tail0.shape, lambda i: (0, 0))],
        out_shape=[jax.ShapeDtypeStruct((rows, d), F32), jax.ShapeDtypeStruct(tail0.shape, F32)],
        scratch_shapes=[pltpu.VMEM(tail0.shape, F32)],
        compiler_params=_params(1),
        name="pool_mixer",
    )(g_mix, tail0, w_pool, scale, x)


def _cast_specs(jobs, steps):
    operands, in_specs, out_specs, out_shapes = [], [], [], []
    for w, layer in jobs:
        _, rows, cols = w.shape
        assert rows % (steps * BF16_ROWS) == 0
        blk = rows // steps
        operands.append(w)
        in_specs.append(pl.BlockSpec((None, blk, cols), lambda i, layer=layer: (layer, i, 0)))
        out_specs.append(pl.BlockSpec((blk, cols), lambda i: (i, 0)))
        out_shapes.append(jax.ShapeDtypeStruct((rows, cols), BF16))
    return operands, in_specs, out_specs, out_shapes


def _run_casts(src_refs, dst_refs):
    for src, dst in zip(src_refs, dst_refs, strict=True):
        dst[...] = src[...].astype(BF16)


def _s5_body(*refs, nb, tc, n_cast):
    x_ref, g_ref, bw_ref, are_ref, aim_ref, cre_ref, cim_ref, dsk_ref, s0re_ref, s0im_ref = refs[:10]
    cast_src, refs = refs[10:10 + n_cast], refs[10 + n_cast:]
    gl_ref, sre_out, sim_out = refs[:3]
    cast_dst, (bu_re, bu_im, st_re, st_im) = refs[3:3 + n_cast], refs[3 + n_cast:]
    _run_casts(cast_src, cast_dst)
    i = pl.program_id(0)

    @pl.when(i == 0)
    def _():
        st_re[...] = s0re_ref[...]
        st_im[...] = s0im_ref[...]

    d = x_ref.shape[-1]
    x = x_ref[...].reshape(tc * nb, d)
    if len(x_ref.shape) == 3:
        x = _permute_rows(x, nb, 'tm')
    h = _rms(x, g_ref[...])
    hb = h.astype(BF16)
    n_state_slab = bw_ref.shape[2] // 2
    ys = []
    for j in range(bw_ref.shape[0]):
        r = jnp.dot(hb[:, j * MXU_DIM:(j + 1) * MXU_DIM], bw_ref[j], preferred_element_type=F32)
        cols = slice(j * n_state_slab, (j + 1) * n_state_slab)
        bu_re[:, cols] = r[:, :n_state_slab]
        bu_im[:, cols] = r[:, n_state_slab:]
        for c in range(n_state_slab // SCAN_LANES):
            lanes = slice(j * n_state_slab + c * SCAN_LANES, j * n_state_slab + (c + 1) * SCAN_LANES)
            ar = jnp.broadcast_to(are_ref[:, lanes], (SUBLANES, SCAN_LANES))
            ai = jnp.broadcast_to(aim_ref[:, lanes], (SUBLANES, SCAN_LANES))
            for rg in range(nb // SUBLANES):
                seqs = slice(rg * SUBLANES, (rg + 1) * SUBLANES)
                sr, si = st_re[seqs, lanes], st_im[seqs, lanes]
                for t in range(tc):
                    rows = slice(t * nb + rg * SUBLANES, t * nb + (rg + 1) * SUBLANES)
                    sr, si = (ar * sr - ai * si + bu_re[rows, lanes], ar * si + ai * sr + bu_im[rows, lanes])
                    bu_re[rows, lanes] = sr
                    bu_im[rows, lanes] = si
                st_re[seqs, lanes] = sr
                st_im[seqs, lanes] = si
        y = jnp.dot(bu_re[:, cols].astype(BF16), cre_ref[j], preferred_element_type=F32)
        ys.append(y + jnp.dot(bu_im[:, cols].astype(BF16), cim_ref[j], preferred_element_type=F32))
    y = jnp.concatenate(ys, axis=1) + dsk_ref[...] * h
    gl_ref[...] = jax.nn.gelu(y, approximate=True).astype(BF16)

    @pl.when(i == pl.num_programs(0) - 1)
    def _():
        sre_out[...] = st_re[...]
        sim_out[...] = st_im[...]


def _s5_core(x, x_layout, g, layer, packed, d_skip, j, s0_re, s0_im, nb, t, tc, cast_jobs):
    rows, d = x.shape
    bw, a_re, a_im, c_re, c_im = packed
    n_lanes = a_re.shape[-1]
    r = tc * nb
    from_bm = x_layout == 'bm'
    assert not from_bm or (r == ROW_TILE and _can_relayout_in_kernel(nb, t))
    cast_ops, cast_in, cast_out, cast_shapes = _cast_specs(cast_jobs, rows // r)
    return pl.pallas_call(
        functools.partial(_s5_body, nb=nb, tc=tc, n_cast=len(cast_jobs)),
        grid=(rows // r,),
        in_specs=[_tile_spec(x_layout, from_bm, nb, d) if from_bm else pl.BlockSpec((r, d), lambda i: (i, 0)),
                  _layer_spec(g.shape, layer), _layer_spec(bw.shape, j), _layer_spec(a_re.shape, j),
                  _layer_spec(a_im.shape, j), _layer_spec(c_re.shape, j), _layer_spec(c_im.shape, j),
                  _layer_spec(d_skip.shape, j),
                  _const_spec((nb, n_lanes)), _const_spec((nb, n_lanes))] + cast_in,
        out_specs=[pl.BlockSpec((r, d), lambda i: (i, 0)),
                   pl.BlockSpec((nb, n_lanes), lambda i: (0, 0)), pl.BlockSpec((nb, n_lanes), lambda i: (0, 0))]
                  + cast_out,
        out_shape=[jax.ShapeDtypeStruct((rows, d), BF16),
                   jax.ShapeDtypeStruct((nb, n_lanes), F32), jax.ShapeDtypeStruct((nb, n_lanes), F32)] + cast_shapes,
        scratch_shapes=[pltpu.VMEM((r, n_lanes), F32), pltpu.VMEM((r, n_lanes), F32),
                        pltpu.VMEM((nb, n_lanes), F32), pltpu.VMEM((nb, n_lanes), F32)],
        compiler_params=_params(1),
        name="s5_core",
    )(_tile_view(x, x_layout, from_bm, nb, t), g, bw, a_re, a_im, c_re, c_im, d_skip, s0_re, s0_im, *cast_ops)


def _s5_weights(a_re, a_im, log_dt, b_re, b_im, c_re, c_im):
    groups, p, hch = b_re.shape
    dt = jnp.exp(log_dt.astype(F32))[:, None]
    lr, li = a_re.astype(F32), a_im.astype(F32)
    mag = jnp.exp(lr * dt)
    ab_re, ab_im = mag * jnp.cos(li * dt), mag * jnp.sin(li * dt)
    den = lr * lr + li * li
    nr, ni = ab_re - 1.0, ab_im
    f_re = (nr * lr + ni * li) / den
    f_im = (ni * lr - nr * li) / den
    br, bi = b_re.astype(F32), b_im.astype(F32)
    bb_re = f_re[..., None] * br - f_im[..., None] * bi
    bb_im = f_re[..., None] * bi + f_im[..., None] * br
    gs = MXU_DIM // hch
    n_slab = groups // gs

    def block_diag(per_group):
        r, c = per_group.shape[1:]
        tiled = jnp.tile(per_group.reshape(n_slab, gs * r, c), (1, 1, gs))
        row_g = lax.broadcasted_iota(jnp.int32, (gs * r, gs * c), 0) // r
        col_g = lax.broadcasted_iota(jnp.int32, (gs * r, gs * c), 1) // c
        return jnp.where(row_g == col_g, tiled, 0.0).astype(BF16)

    to_hp = lambda bb: bb.transpose(0, 2, 1)
    bw = jnp.concatenate([block_diag(to_hp(bb_re)), block_diag(to_hp(bb_im))], axis=2)
    cw_re = block_diag(c_re.astype(F32).transpose(0, 2, 1))
    cw_im = block_diag(-c_im.astype(F32).transpose(0, 2, 1))
    return bw, ab_re.reshape(1, groups * p), ab_im.reshape(1, groups * p), cw_re, cw_im


def _ret_in_body(*refs, d, qk, n_cast):
    x_ref, g_ref, w_ref, cos_ref, sin_ref = refs[:5]
    q_ref, k_ref, v_ref, gate_ref = refs[5 + n_cast:9 + n_cast]
    _run_casts(refs[5:5 + n_cast], refs[9 + n_cast:])
    h = _rms(x_ref[...], g_ref[...]).astype(BF16)
    cos, sin = cos_ref[...], sin_ref[...]
    half = qk // 2

    def rot(z):
        outs = []
        for hd in range(d // qk):
            z1 = z[:, hd * qk:hd * qk + half]
            z2 = z[:, hd * qk + half:(hd + 1) * qk]
            outs += [z1 * cos - z2 * sin, z1 * sin + z2 * cos]
        return jnp.concatenate(outs, axis=1)

    q = jnp.dot(h, w_ref[:, 0:d], preferred_element_type=F32)
    q_ref[...] = (rot(q) * (qk ** -0.5)).astype(BF16)
    k = jnp.dot(h, w_ref[:, d:2 * d], preferred_element_type=F32)
    k_ref[...] = rot(k).astype(BF16)
    v_ref[...] = jnp.dot(h, w_ref[:, 2 * d:4 * d], preferred_element_type=F32).astype(BF16)
    gate_ref[...] = jnp.dot(h, w_ref[:, 4 * d:6 * d], preferred_element_type=F32).astype(BF16)


def _ret_in(x, g, layer, w_in, cos_tab, sin_tab, cast_jobs):
    rows, d = x.shape
    qk = d // RET_HEADS
    tab_blocks = cos_tab.shape[0] // ROW_TILE
    row_spec = lambda width: pl.BlockSpec((ROW_TILE, width), lambda i: (i, 0))
    tab_spec = pl.BlockSpec((ROW_TILE, qk // 2), lambda i: (i % tab_blocks, 0))
    cast_ops, cast_in, cast_out, cast_shapes = _cast_specs(cast_jobs, rows // ROW_TILE)
    return pl.pallas_call(
        functools.partial(_ret_in_body, d=d, qk=qk, n_cast=len(cast_jobs)),
        grid=(rows // ROW_TILE,),
        in_specs=[row_spec(d), _layer_spec(g.shape, layer), _const_spec(w_in.shape), tab_spec, tab_spec] + cast_in,
        out_specs=[row_spec(d), row_spec(d), row_spec(2 * d), row_spec(2 * d)] + cast_out,
        out_shape=[jax.ShapeDtypeStruct((rows, d), BF16), jax.ShapeDtypeStruct((rows, d), BF16),
                   jax.ShapeDtypeStruct((rows, 2 * d), BF16), jax.ShapeDtypeStruct((rows, 2 * d), BF16)] + cast_shapes,
        compiler_params=_params(1),
        name="ret_in",
    )(x, g, w_in, cos_tab, sin_tab, *cast_ops)


def _ret_core_body(q_ref, k_ref, v_ref, dmask_ref, qdec_ref, kdec_ref, cdec_ref, *rest,
                   lb, lp, qk, vd, seqs, chunks):
    o_ref, s_ref = rest[-2:]
    c = pl.program_id(1)

    @pl.when(c == 0)
    def _():
        s_ref[...] = rest[0][...] if len(rest) == 3 else jnp.zeros(s_ref.shape, s_ref.dtype)

    def padded(z):
        if lb == lp:
            return z
        return jnp.concatenate([z, jnp.zeros((lp - lb, z.shape[1]), z.dtype)], axis=0)

    for blk in range(seqs * chunks):
        b = blk // chunks
        rows = slice(blk * lb, (blk + 1) * lb)
        q, k, v = padded(q_ref[rows, :]), padded(k_ref[rows, :]), padded(v_ref[rows, :])
        for hd in range(RET_HEADS):
            qh = q[:, hd * qk:(hd + 1) * qk]
            kh = k[:, hd * qk:(hd + 1) * qk]
            kf = kh.astype(F32)
            vh = v[:, hd * vd:(hd + 1) * vd]
            s_old = s_ref[b, hd]
            scores = lax.dot_general(qh, kh, (((1,), (1,)), ((), ())), preferred_element_type=F32) * dmask_ref[hd]
            o = jnp.dot(scores.astype(BF16), vh, preferred_element_type=F32)
            o = o + qdec_ref[hd] * jnp.dot(qh, s_old.astype(BF16), preferred_element_type=F32)
            kd = (kf * kdec_ref[hd]).astype(BF16)
            s_ref[b, hd] = cdec_ref[hd] * s_old + lax.dot_general(kd, vh, (((0,), (0,)), ((), ())),
                                                                  preferred_element_type=F32)
            o_ref[rows, hd * vd:(hd + 1) * vd] = o[:lb]


def _ret_core(q, k, v, s0, consts, n, nc, lb):
    rows, d = q.shape
    qk, vd = d // RET_HEADS, 2 * d // RET_HEADS
    dmask, qdec, kdec, cdec = consts
    lp = dmask.shape[1]
    chunks = math.gcd(nc, max(ROW_TILE // lb, 1))
    seqs = math.gcd(n, RET_SEQS_PER_STEP) if chunks == nc else 1
    nc_blocks = nc // chunks
    row_spec = lambda width: pl.BlockSpec((seqs * chunks * lb, width), lambda b, c: (b * nc_blocks + c, 0))
    s_spec = pl.BlockSpec((seqs, RET_HEADS, qk, vd), lambda b, c: (b, 0, 0, 0))
    return pl.pallas_call(
        functools.partial(_ret_core_body, lb=lb, lp=lp, qk=qk, vd=vd, seqs=seqs, chunks=chunks),
        grid=(n // seqs, nc_blocks),
        in_specs=[row_spec(d), row_spec(d), row_spec(2 * d),
                  _const_spec(dmask.shape), _const_spec(qdec.shape), _const_spec(kdec.shape), _const_spec(cdec.shape)]
                 + ([] if s0 is None else [s_spec]),
        out_specs=[row_spec(2 * d), s_spec],
        out_shape=[jax.ShapeDtypeStruct((rows, 2 * d), F32), jax.ShapeDtypeStruct((n, RET_HEADS, qk, vd), F32)],
        compiler_params=_params(2),
        name="ret_core",
    )(q, k, v, dmask, qdec, kdec, cdec, *(() if s0 is None else (s0,)))


def _ret_consts(l_real, l_pad, qk, vd):
    log_g = jnp.log1p(-jnp.exp2(-5.0 - jnp.arange(RET_HEADS, dtype=F32)))
    idx = jnp.arange(l_real, dtype=F32)
    diff = idx[:, None] - idx[None, :]
    dmask = jnp.exp(jnp.maximum(diff, 0.0)[None] * log_g[:, None, None]) * (diff >= 0)[None]
    q_dec = jnp.exp((idx + 1.0)[:, None] * log_g[None, :])
    k_dec = jnp.exp((l_real - 1.0 - idx)[:, None] * log_g[None, :])
    chunk_dec = jnp.exp(l_real * log_g)
    pad = l_pad - l_real
    dmask = jnp.pad(dmask, ((0, 0), (0, pad), (0, pad)))
    q_dec = jnp.pad(q_dec.T, ((0, 0), (0, pad)))
    k_dec = jnp.pad(k_dec.T, ((0, 0), (0, pad)))
    qdec = jnp.broadcast_to(q_dec[:, :, None], (RET_HEADS, l_pad, vd))
    kdec = jnp.broadcast_to(k_dec[:, :, None], (RET_HEADS, l_pad, qk))
    cdec = jnp.broadcast_to(chunk_dec[:, None, None], (RET_HEADS, 1, vd))
    return dmask, qdec, kdec, cdec


def _rope_tables(pos0, t, qk):
    half = qk // 2
    inv = ROPE_BASE ** (-jnp.arange(half, dtype=F32) / half)
    pos = pos0 + jnp.arange(t)
    ang = pos.astype(F32)[:, None] * inv[None, :]
    return jnp.cos(ang), jnp.sin(ang)


def _relayout(x, n, t, src, dst):
    if src == dst:
        return x
    lead = (n, t) if src == 'bm' else (t, n)
    return x.reshape(lead + (-1,)).transpose(1, 0, 2).reshape(n * t, -1)


def _mixer_layout(layer):
    return 'bm' if layer % 3 == 1 else 'tm'


def _cast_jobs(wts, layer, steps):
    depth = wts['mlp_w_up'].shape[0]
    mlp_layers = [layer]
    while mlp_layers[-1] + 1 < depth and (mlp_layers[-1] + 1) % 3 == 2:
        mlp_layers.append(mlp_layers[-1] + 1)
    keys = [(name, l) for l in mlp_layers for name in ('mlp_w_up', 'mlp_w_down')]
    if layer % 3 == 0:
        keys.append(('s5_w_glu', layer // 3))
        if layer + 1 < depth and (layer + 1) % 3 == 1:
            keys.append(('ret_w_in', (layer + 1) // 3))
    elif layer % 3 == 1:
        keys.append(('ret_w_out', layer // 3))
    keys = [k for k in keys if k not in wts['bf16'] and wts[k[0]].shape[1] % (steps * BF16_ROWS) == 0]
    return keys, [(wts[name], idx) for name, idx in keys]


def _bf16_weight(wts, name, idx):
    if (name, idx) not in wts['bf16']:
        wts['bf16'][(name, idx)] = wts[name][idx].astype(BF16)
    return wts['bf16'][(name, idx)]


def _trunk(x, pos0, st_re, st_im, st_ret, st_pool, wts):
    n, t, d = x.shape
    rows = n * t
    depth = wts['norm_mix'].shape[0]
    qk, vd = d // RET_HEADS, 2 * d // RET_HEADS
    tc = min(max(ROW_TILE // n, 1), t)
    in_kernel_relayout = _can_relayout_in_kernel(n, t)
    new_re, new_im, new_ret, new_pool = [], [], [], []

    xb, layout = x.reshape(rows, d), 'bm'
    for layer in range(depth):
        kind, j = layer % 3, layer // 3
        want = _mixer_layout(layer)
        if layout != want and not (kind == 0 and in_kernel_relayout):
            xb, layout = _relayout(xb, n, t, layout, want), want
        if kind == 1:
            l_real = math.gcd(t, RET_CHUNK)
            nc = t // l_real
            lb = max(l_real, BF16_ROWS)
            cos, sin = _rope_tables(pos0, t, qk)
            if t < ROW_TILE:
                cos, sin = jnp.tile(cos, (ROW_TILE // t, 1)), jnp.tile(sin, (ROW_TILE // t, 1))
            cast_keys, cast_jobs = _cast_jobs(wts, layer, rows // ROW_TILE)
            q, k, v, gate, *casts = _ret_in(xb, wts['norm_mix'], layer, _bf16_weight(wts, 'ret_w_in', j), cos, sin,
                                            cast_jobs)
            if lb != l_real:
                pad = lambda z: jnp.pad(z.reshape(n * nc, l_real, -1), ((0, 0), (0, lb - l_real), (0, 0))
                                        ).reshape(n * nc * lb, -1)
                q, k, v = pad(q), pad(k), pad(v)
            consts = _ret_consts(l_real, max(l_real, RET_MIN_CHUNK_ROWS), qk, vd)
            o, s_new = _ret_core(q, k, v, None if st_ret is None else st_ret[j], consts, n, nc, lb)
            if lb != l_real:
                o = o.reshape(n * nc, lb, -1)[:, :l_real].reshape(rows, -1)
            proj, mixer = 'gated', (o, gate, ('ret_w_out', j), j)
            new_ret.append(s_new)
        elif kind == 0:
            cast_keys, cast_jobs = _cast_jobs(wts, layer, rows // (tc * n))
            gl, s_re, s_im, *casts = _s5_core(xb, layout, wts['norm_mix'], layer, wts['s5_packed'], wts['s5_d'], j,
                                              st_re[j].reshape(n, -1), st_im[j].reshape(n, -1), n, t, tc, cast_jobs)
            proj, mixer = 'glu', (gl, ('s5_w_glu', j), j)
            new_re.append(s_re.reshape(st_re[j].shape))
            new_im.append(s_im.reshape(st_im[j].shape))
        else:
            cast_keys, casts = [], []
            buf = st_pool[j]
            buf_t = buf.transpose(1, 0, 2).reshape(POOL_TAIL * n, d)
            if buf_t.size * buf_t.dtype.itemsize <= POOL_FUSE_MAX_TAIL_BYTES:
                proj, mixer = 'pool', (wts['norm_mix'], buf_t, wts['pool_w'], wts['pool_scale'], j)
            else:
                xb, tail = _pool(xb, wts['norm_mix'], layer, buf_t, wts['pool_w'], wts['pool_scale'], j, n, pos0)
                new_pool.append(tail.reshape(POOL_TAIL, n, d).transpose(1, 0, 2))
                proj, mixer = None, ()
        wts['bf16'].update(zip(cast_keys, casts, strict=True))
        if proj in ('glu', 'gated'):
            mixer = mixer[:-2] + (_bf16_weight(wts, *mixer[-2]), j)
        final = layer == depth - 1
        wanted = 'bm' if final else _mixer_layout(layer + 1)
        dst = wanted if in_kernel_relayout else layout
        xb, *tail = _mlp(proj, mixer, want, xb, layout, wts['norm_ffn'], _bf16_weight(wts, 'mlp_w_up', layer),
                         _bf16_weight(wts, 'mlp_w_down', layer), layer, wts['norm_final'], final, n, t, dst, pos0)
        if tail:
            new_pool.append(tail[0].reshape(POOL_TAIL, n, d).transpose(1, 0, 2))
        layout = dst
    y = _relayout(xb, n, t, layout, 'bm').reshape(n, t, d)
    return y, jnp.stack(new_re), jnp.stack(new_im), jnp.stack(new_ret), jnp.stack(new_pool)


def kernel(x_prompt, x_sample, state_s5_re, state_s5_im, state_ret, state_pool, norm_mix, norm_ffn, norm_final,
           s5_a_re, s5_a_im, s5_log_dt, s5_b_re, s5_b_im, s5_c_re, s5_c_im, s5_d, s5_w_glu, ret_w_in, ret_w_out,
           pool_w, pool_scale, mlp_w_up, mlp_w_down):
    nb = x_prompt.shape[0]
    depth, d = norm_mix.shape
    wts = dict(
        norm_mix=norm_mix.reshape(depth, 1, d), norm_ffn=norm_ffn.reshape(depth, 1, d), norm_final=norm_final,
        s5_d=s5_d.reshape(s5_d.shape[0], 1, d),
        s5_packed=jax.vmap(_s5_weights)(s5_a_re, s5_a_im, s5_log_dt, s5_b_re, s5_b_im, s5_c_re, s5_c_im),
        s5_w_glu=s5_w_glu, ret_w_in=ret_w_in, ret_w_out=ret_w_out, mlp_w_up=mlp_w_up, mlp_w_down=mlp_w_down,
        bf16={},
        pool_w=pool_w.astype(BF16), pool_scale=pool_scale.reshape(pool_scale.shape[0], 1, d))
    z_re = jnp.zeros((state_s5_re.shape[0], nb) + state_s5_re.shape[2:], state_s5_re.dtype)
    z_im = jnp.zeros((state_s5_im.shape[0], nb) + state_s5_im.shape[2:], state_s5_im.dtype)
    z_pool = jnp.zeros((state_pool.shape[0], nb) + state_pool.shape[2:], state_pool.dtype)
    y_p, p_re, p_im, p_ret, p_pool = _trunk(x_prompt, 0, z_re, z_im, None, z_pool, wts)
    y_s, s_re, s_im, s_ret, s_pool = _trunk(x_sample, PAST_LEN, state_s5_re, state_s5_im, state_ret, state_pool, wts)
    return (y_p, y_s, p_re, p_im, p_ret, p_pool, s_re, s_im, s_ret, s_pool)
```
